```python
import jax, jax.numpy as jnp
from jax import lax
import numpy as np

D_MODEL = 1024
BATCH = 8
SEQ = 2048
DEPTH = 1
DEC_BATCH = 128
DEC_SEQ = 8
PAST_LEN = 8192
PAGE_SIZE = 128

HEAD_DIM = 64
ATTN_WIDTH = D_MODEL // 2
N_HEADS = ATTN_WIDTH // HEAD_DIM
ROT_DIM = HEAD_DIM // 4
ROPE_THETA = 500000.0
MOBA_BLOCK = 256
MOBA_TOPK = 3
QUERY_CHUNK = 64
CONV_WIDTH = 31
CONV_CH = D_MODEL // 2
N_BRANCH = 2
EPS = 1e-6
IN_SIZES = (CONV_CH, CONV_CH, CONV_CH, ATTN_WIDTH, ATTN_WIDTH, ATTN_WIDTH, ATTN_WIDTH, D_MODEL, D_MODEL)
IN_COLS = sum(IN_SIZES)

kernel_name = 'conv_moba_parallel_gated_decoder_step'

F32 = jnp.float32


def _rmsnorm(x, g):
    x32 = x.astype(F32)
    r = x32 * lax.rsqrt(jnp.mean(x32 * x32, axis=-1, keepdims=True) + EPS)
    return (r * g.astype(F32)).astype(x.dtype)


def _layernorm(x, g, b):
    x32 = x.astype(F32)
    xc = x32 - jnp.mean(x32, axis=-1, keepdims=True)
    var = jnp.mean(xc * xc, axis=-1, keepdims=True)
    return (xc * lax.rsqrt(var + EPS) * g.astype(F32) + b.astype(F32)).astype(x.dtype)


def _rope(x, pos):
    inv = ROPE_THETA ** (-(jnp.arange(0, ROT_DIM, 2, dtype=F32) / ROT_DIM))
    ang = pos.astype(F32)[:, None] * inv[None, :]
    cos = jnp.cos(ang)[:, None, :]
    sin = jnp.sin(ang)[:, None, :]
    xr = x[..., :ROT_DIM].astype(F32)
    x1, x2 = xr[..., :ROT_DIM // 2], xr[..., ROT_DIM // 2:]
    rot = jnp.concatenate([x1 * cos - x2 * sin, x1 * sin + x2 * cos], axis=-1).astype(x.dtype)
    return jnp.concatenate([rot, x[..., ROT_DIM:]], axis=-1)


def _mixer_in(x, c, norm_g, w_ada, b_ada, w_in):
    ada = jax.nn.silu(c) @ w_ada + b_ada
    shift, scale, gate = jnp.split(ada, 3, axis=-1)
    h = _rmsnorm(x, norm_g) * (1 + scale[:, None, :]) + shift[:, None, :]
    splits = tuple(int(o) for o in np.cumsum(IN_SIZES)[:-1])
    parts = jnp.split(h @ w_in, splits, axis=-1)
    return parts, gate


def _conv_branch(glu_hist, conv_gate, conv_w, conv_b, ln_g, ln_b):
    y = lax.conv_general_dilated(glu_hist, conv_w[:, None, :], (1,), 'VALID',
                                 dimension_numbers=('NWC', 'WIO', 'NWC'),
                                 feature_group_count=CONV_CH) + conv_b
    y = jax.nn.silu(_layernorm(y, ln_g, ln_b))
    return y * jax.nn.silu(conv_gate)


def _mixer_out(x, y_conv, y_attn, gm_conv, gm_attn, gate, w_branch, w_out):
    merged = (jax.nn.sigmoid(gm_conv) * (y_conv @ w_branch[0])
              + jax.nn.sigmoid(gm_attn) * (y_attn @ w_branch[1]))
    return x + gate[:, None, :] * (merged @ w_out)


def _moba_attend(q, k_sel, v_sel, sel_valid, k_own, v_own, own_valid):
    n, nq, h, _ = q.shape
    scale = HEAD_DIM ** -0.5
    s_own = jnp.einsum('nqhd,nkhd->nqhk', q, k_own).astype(F32) * scale
    s_own = jnp.where(own_valid[None, :, None, :], s_own, -jnp.inf)
    if k_sel is None:
        p = jax.nn.softmax(s_own, axis=-1).astype(v_own.dtype)
        return jnp.einsum('nqhk,nkhd->nqhd', p, v_own)
    n_sel, blk = k_sel.shape[3], k_sel.shape[4]
    s_sel = jnp.einsum('nqhd,nqhjld->nqhjl', q, k_sel).astype(F32) * scale
    s_sel = jnp.where(sel_valid[:, None], s_sel, -jnp.inf).reshape(n, nq, h, n_sel * blk)
    p = jax.nn.softmax(jnp.concatenate([s_sel, s_own], axis=-1), axis=-1).astype(v_own.dtype)
    p_sel = p[..., :n_sel * blk].reshape(n, nq, h, n_sel, blk)
    p_own = p[..., n_sel * blk:]
    return (jnp.einsum('nqhjl,nqhjld->nqhd', p_sel, v_sel)
            + jnp.einsum('nqhk,nkhd->nqhd', p_own, v_own))


def _prompt_moba(q, k, v):
    b, s, h, d = q.shape
    nb = -(-s // MOBA_BLOCK)
    pad = nb * MOBA_BLOCK - s
    kp = jnp.pad(k, ((0, 0), (0, pad), (0, 0), (0, 0)))
    vp = jnp.pad(v, ((0, 0), (0, pad), (0, 0), (0, 0)))
    n_sel = min(MOBA_TOPK, nb - 1)
    kb_t = kp.reshape(b, nb, MOBA_BLOCK, h, d).transpose(0, 3, 1, 2, 4)
    vb_t = vp.reshape(b, nb, MOBA_BLOCK, h, d).transpose(0, 3, 1, 2, 4)
    k_mean = jnp.mean(kb_t.astype(F32), axis=3)
    bi = jnp.arange(b)[:, None, None, None]
    hi = jnp.arange(h)[None, None, :, None]
    n_chunks = s // QUERY_CHUNK
    q_chunks = q.reshape(b, n_chunks, QUERY_CHUNK, h, d).transpose(1, 0, 2, 3, 4)

    def one_chunk(args):
        q_c, ci = args
        start = ci * QUERY_CHUNK
        blk = start // MOBA_BLOCK
        q_pos = start + jnp.arange(QUERY_CHUNK)
        k_pos = blk * MOBA_BLOCK + jnp.arange(MOBA_BLOCK)
        own_valid = k_pos[None, :] <= q_pos[:, None]
        k_own = lax.dynamic_slice_in_dim(kp, blk * MOBA_BLOCK, MOBA_BLOCK, axis=1)
        v_own = lax.dynamic_slice_in_dim(vp, blk * MOBA_BLOCK, MOBA_BLOCK, axis=1)
        if n_sel == 0:
            return _moba_attend(q_c, None, None, None, k_own, v_own, own_valid)
        s_blk = jnp.einsum('bqhd,bhnd->bqhn', q_c.astype(F32), k_mean)
        s_blk = jnp.where(jnp.arange(nb) < blk, s_blk, -jnp.inf)
        _, idx = lax.top_k(s_blk, n_sel)
        k_sel = kb_t[bi, hi, idx]
        v_sel = vb_t[bi, hi, idx]
        return _moba_attend(q_c, k_sel, v_sel, jnp.arange(n_sel) < blk, k_own, v_own, own_valid)

    out = lax.map(one_chunk, (q_chunks, jnp.arange(n_chunks)))
    return out.transpose(1, 0, 2, 3, 4).reshape(b, s, h, d)


def _sample_moba(q, k_new, v_new, cache_k, cache_v, layer, page_table):
    db, ds, h, d = q.shape
    n_pages = page_table.shape[1]
    past_len = n_pages * PAGE_SIZE
    ppb = MOBA_BLOCK // PAGE_SIZE
    npb = past_len // MOBA_BLOCK
    n_own_pages = n_pages - npb * ppb
    own_pages = page_table[:, npb * ppb:]
    r = n_own_pages * PAGE_SIZE
    k_own = jnp.concatenate([cache_k[layer, own_pages].reshape(db, r, h, d), k_new], axis=1)
    v_own = jnp.concatenate([cache_v[layer, own_pages].reshape(db, r, h, d), v_new], axis=1)
    tri = jnp.arange(ds)[None, :] <= jnp.arange(ds)[:, None]
    own_valid = jnp.concatenate([jnp.ones((ds, r), bool), tri], axis=1)
    n_sel = min(MOBA_TOPK, npb)
    if n_sel == 0:
        return _moba_attend(q, None, None, None, k_own, v_own, own_valid)
    past_pages = page_table[:, :npb * ppb]

    def block_means(pt):
        rows = cache_k[layer, pt].astype(F32)
        return jnp.mean(rows.reshape(npb, MOBA_BLOCK, h, d), axis=1)

    k_mean = lax.map(block_means, past_pages)
    s_blk = jnp.einsum('bqhd,bnhd->bqhn', q.astype(F32), k_mean)
    _, idx = lax.top_k(s_blk, n_sel)
    bi = jnp.arange(db)[:, None, None, None, None]
    phys = page_table[bi, idx[..., None] * ppb + jnp.arange(ppb)]
    hi = jnp.arange(h)[None, :, None, None]
    sel_valid = jnp.ones((n_sel,), bool)

    def one_token(args):
        q_i, phys_i, valid_i = args
        k_sel = cache_k[layer, phys_i, :, hi, :].reshape(db, 1, h, n_sel, MOBA_BLOCK, d)
        v_sel = cache_v[layer, phys_i, :, hi, :].reshape(db, 1, h, n_sel, MOBA_BLOCK, d)
        out = _moba_attend(q_i[:, None], k_sel, v_sel, sel_valid, k_own, v_own, valid_i[None])
        return out[:, 0]

    out = lax.map(one_token, (q.transpose(1, 0, 2, 3), phys.transpose(1, 0, 2, 3, 4), own_valid))
    return out.transpose(1, 0, 2, 3)


def setup_inputs(seed: int = 0) -> dict:
    key = jax.random.key(seed)
    ks = jax.random.split(key, 20)
    n_pages = PAST_LEN // PAGE_SIZE
    n_used = DEC_BATCH * n_pages
    n_phys = n_used + (n_used + 3) // 4
    page_table = jax.random.permutation(ks[0], n_phys)[:n_used].reshape(DEC_BATCH, n_pages).astype(jnp.int32)

    def nrm(k, shape, s):
        return s * jax.random.normal(k, shape, F32)

    return {
        'x_prompt': nrm(ks[1], (BATCH, SEQ, D_MODEL), 1.0),
        'x_sample': nrm(ks[2], (DEC_BATCH, DEC_SEQ, D_MODEL), 1.0),
        'cache_k': nrm(ks[3], (DEPTH, n_phys, PAGE_SIZE, N_HEADS, HEAD_DIM), 1.0),
        'cache_v': nrm(ks[4], (DEPTH, n_phys, PAGE_SIZE, N_HEADS, HEAD_DIM), 1.0),
        'state_conv': nrm(ks[5], (DEPTH, DEC_BATCH, CONV_WIDTH - 1, CONV_CH), 0.5),
        'page_table': page_table,
        'c_prompt': nrm(ks[6], (BATCH, D_MODEL), 1.0),
        'c_sample': nrm(ks[7], (DEC_BATCH, D_MODEL), 1.0),
        'norm_g': 1.0 + nrm(ks[8], (DEPTH, D_MODEL), 0.02),
        'w_ada': nrm(ks[9], (DEPTH, D_MODEL, 3 * D_MODEL), D_MODEL ** -0.5),
        'b_ada': nrm(ks[10], (DEPTH, 3 * D_MODEL), 0.02),
        'w_in': nrm(ks[11], (DEPTH, D_MODEL, IN_COLS), D_MODEL ** -0.5),
        'conv_w': nrm(ks[12], (DEPTH, CONV_WIDTH, CONV_CH), CONV_WIDTH ** -0.5),
        'conv_b': nrm(ks[13], (DEPTH, CONV_CH), 0.02),
        'conv_ln_g': 1.0 + nrm(ks[14], (DEPTH, CONV_CH), 0.02),
        'conv_ln_b': nrm(ks[15], (DEPTH, CONV_CH), 0.02),
        'w_branch': nrm(ks[16], (DEPTH, N_BRANCH, ATTN_WIDTH, D_MODEL), ATTN_WIDTH ** -0.5),
        'w_out': nrm(ks[17], (DEPTH, D_MODEL, D_MODEL), D_MODEL ** -0.5),
        'final_g': 1.0 + nrm(ks[18], (D_MODEL,), 0.02),
    }


def reference(x_prompt, x_sample, cache_k, cache_v, state_conv, page_table, c_prompt, c_sample,
              norm_g, w_ada, b_ada, w_in, conv_w, conv_b, conv_ln_g, conv_ln_b, w_branch, w_out, final_g):
    b, s = x_prompt.shape[:2]
    db, ds = x_sample.shape[:2]
    past_len = page_table.shape[1] * PAGE_SIZE
    pos_p = jnp.arange(s)
    pos_s = past_len + jnp.arange(ds)
    xp, xs = x_prompt, x_sample
    k_p, v_p, conv_p, k_s, v_s, conv_s = [], [], [], [], [], []
    for l in range(DEPTH):
        (ga, gb, cg, q, k, v, ag, gmc, gma), gate = _mixer_in(xp, c_prompt, norm_g[l], w_ada[l], b_ada[l], w_in[l])
        glu = ga * jax.nn.sigmoid(gb)
        glu_hist = jnp.pad(glu, ((0, 0), (CONV_WIDTH - 1, 0), (0, 0)))
        y_conv = _conv_branch(glu_hist, cg, conv_w[l], conv_b[l], conv_ln_g[l], conv_ln_b[l])
        q = _rope(q.reshape(b, s, N_HEADS, HEAD_DIM), pos_p)
        k = _rope(k.reshape(b, s, N_HEADS, HEAD_DIM), pos_p)
        v = v.reshape(b, s, N_HEADS, HEAD_DIM)
        y_attn = _prompt_moba(q, k, v).reshape(b, s, ATTN_WIDTH) * jax.nn.silu(ag)
        xp = _mixer_out(xp, y_conv, y_attn, gmc, gma, gate, w_branch[l], w_out[l])
        k_p.append(k.reshape(b, s // PAGE_SIZE, PAGE_SIZE, N_HEADS, HEAD_DIM))
        v_p.append(v.reshape(b, s // PAGE_SIZE, PAGE_SIZE, N_HEADS, HEAD_DIM))
        conv_p.append(glu_hist[:, -(CONV_WIDTH - 1):])
        (ga, gb, cg, q, k, v, ag, gmc, gma), gate = _mixer_in(xs, c_sample, norm_g[l], w_ada[l], b_ada[l], w_in[l])
        glu = ga * jax.nn.sigmoid(gb)
        glu_hist = jnp.concatenate([state_conv[l].astype(glu.dtype), glu], axis=1)
        y_conv = _conv_branch(glu_hist, cg, conv_w[l], conv_b[l], conv_ln_g[l], conv_ln_b[l])
        q = _rope(q.reshape(db, ds, N_HEADS, HEAD_DIM), pos_s)
        k = _rope(k.reshape(db, ds, N_HEADS, HEAD_DIM), pos_s)
        v = v.reshape(db, ds, N_HEADS, HEAD_DIM)
        y_attn = _sample_moba(q, k, v, cache_k, cache_v, l, page_table).reshape(db, ds, ATTN_WIDTH) * jax.nn.silu(ag)
        xs = _mixer_out(xs, y_conv, y_attn, gmc, gma, gate, w_branch[l], w_out[l])
        k_s.append(k)
        v_s.append(v)
        conv_s.append(glu_hist[:, -(CONV_WIDTH - 1):])
    y_prompt = _rmsnorm(xp, final_g)
    y_sample = _rmsnorm(xs, final_g)
    return (y_prompt, y_sample, jnp.stack(k_p), jnp.stack(v_p), jnp.stack(conv_p),
            jnp.stack(k_s), jnp.stack(v_s), jnp.stack(conv_s))
```

```python
import functools

import jax
import jax.numpy as jnp
from jax import lax
from jax.experimental import pallas as pl
from jax.experimental.pallas import tpu as pltpu

F32 = jnp.float32
BF16 = jnp.bfloat16

D_MODEL = 1024
HEAD_DIM = 64
N_HEADS = 8
ATTN_WIDTH = N_HEADS * HEAD_DIM
CONV_CH = 512
ROT_DIM = HEAD_DIM // 4
ROPE_THETA = 500000.0
MOBA_BLOCK = 256
MOBA_TOPK = 3
CONV_WIDTH = 31
PAGE_SIZE = 128
EPS = 1e-6
IN_SIZES = (CONV_CH, CONV_CH, CONV_CH, ATTN_WIDTH, ATTN_WIDTH, ATTN_WIDTH, ATTN_WIDTH, D_MODEL, D_MODEL)
IN_COLS = sum(IN_SIZES)
NEG = -1e30

ROW_TILE = 256
CONV_TILE = 128
CONV_SUB = 32
HIST_PAD = 32
SAMPLE_BLOCKS_PER_STEP = 4
VMEM_LIMIT = 48 * 1024 * 1024


def _silu(x):
    return x * jax.nn.sigmoid(x)


def _dot(a, b):
    return jnp.dot(a, b, preferred_element_type=F32)


def _dot_nt(a, b, precision=None):
    return lax.dot_general(a, b, (((1,), (1,)), ((), ())), precision=precision,
                           preferred_element_type=F32)


def _ada_body(c_ref, w_ref, b_ref, o_ref):
    a = _silu(c_ref[...]).astype(BF16)
    o_ref[...] = _dot(a, w_ref[...].astype(BF16)) + b_ref[...]


def _ada(c_all, w_ada, b_ada):
    n = c_all.shape[0]
    return pl.pallas_call(
        _ada_body,
        grid=(3,),
        in_specs=[pl.BlockSpec((n, D_MODEL), lambda j: (0, 0)),
                  pl.BlockSpec((D_MODEL, D_MODEL), lambda j: (0, j)),
                  pl.BlockSpec((1, D_MODEL), lambda j: (0, j))],
        out_specs=pl.BlockSpec((n, D_MODEL), lambda j: (0, j)),
        out_shape=jax.ShapeDtypeStruct((n, 3 * D_MODEL), F32),
        compiler_params=pltpu.CompilerParams(dimension_semantics=("arbitrary",),
                                             vmem_limit_bytes=VMEM_LIMIT),
        name="ada",
    )(c_all, w_ada, b_ada.reshape(1, 3 * D_MODEL))


def _rope_rows(x, cos, sin_lo, sin_hi):
    n = x.shape[-1]
    half = ROT_DIM // 2
    return x * cos + pltpu.roll(x, n - half, 1) * sin_lo + pltpu.roll(x, half, 1) * sin_hi


def _mixer_in_body(x_ref, scale_ref, shift_ref, g_ref, w_ref, cos_ref, slo_ref, shi_ref,
                   glu_ref, cg_ref, q_ref, k_ref, v_ref, ag_ref, gmc_ref, gma_ref):
    gb, tb, d = x_ref.shape
    m = gb * tb
    x = x_ref[...]
    r = x * lax.rsqrt(jnp.mean(x * x, axis=-1, keepdims=True) + EPS)
    h = (r * g_ref[...]) * (1.0 + scale_ref[...]) + shift_ref[...]
    hb = h.reshape(m, d).astype(BF16)

    def seg(idx):
        lo = sum(IN_SIZES[:idx])
        return _dot(hb, w_ref[:, lo:lo + IN_SIZES[idx]])

    def put(ref, val):
        ref[...] = val.reshape(ref.shape)

    put(glu_ref, seg(0) * jax.nn.sigmoid(seg(1)))
    put(cg_ref, seg(2))
    cos, slo, shi = cos_ref[...], slo_ref[...], shi_ref[...]
    put(q_ref, _rope_rows(seg(3), cos, slo, shi))
    put(k_ref, _rope_rows(seg(4), cos, slo, shi))
    put(v_ref, seg(5))
    put(ag_ref, seg(6))
    put(gmc_ref, seg(7))
    put(gma_ref, seg(8))


def _mixer_in(x3, scale, shift, norm_g, w_in_bf, tables, gb, tb):
    g, t, d = x3.shape
    m = gb * tb
    grid = (g // gb, t // tb)
    xmap = lambda i, j: (i, j, 0)
    bmap = lambda i, j: (i, 0, 0)
    cmap = lambda i, j: (0, 0)
    tmap = lambda i, j: (j, 0)
    widths = (CONV_CH, CONV_CH, ATTN_WIDTH, ATTN_WIDTH, ATTN_WIDTH, ATTN_WIDTH, D_MODEL, D_MODEL)
    return pl.pallas_call(
        _mixer_in_body,
        grid=grid,
        in_specs=[pl.BlockSpec((gb, tb, d), xmap),
                  pl.BlockSpec((gb, 1, d), bmap),
                  pl.BlockSpec((gb, 1, d), bmap),
                  pl.BlockSpec((1, d), cmap),
                  pl.BlockSpec((d, IN_COLS), cmap, pipeline_mode=pl.Buffered(1)),
                  pl.BlockSpec((m, ATTN_WIDTH), tmap),
                  pl.BlockSpec((m, ATTN_WIDTH), tmap),
                  pl.BlockSpec((m, ATTN_WIDTH), tmap)],
        out_specs=[pl.BlockSpec((gb, tb, w), xmap) for w in widths],
        out_shape=[jax.ShapeDtypeStruct((g, t, w), F32) for w in widths],
        compiler_params=pltpu.CompilerParams(dimension_semantics=("parallel", "parallel"),
                                             vmem_limit_bytes=VMEM_LIMIT),
        name="mixer_in",
    )(x3, scale, shift, norm_g, w_in_bf, *tables)


def _rope_tables(pos):
    inv = ROPE_THETA ** (-(jnp.arange(0, ROT_DIM, 2, dtype=F32) / ROT_DIM))
    ang = pos.astype(F32)[:, None] * inv[None, :]
    cos, sin = jnp.cos(ang), jnp.sin(ang)
    half = ROT_DIM // 2
    n = pos.shape[0]
    pad = jnp.zeros((n, HEAD_DIM - ROT_DIM), F32)
    zero = jnp.zeros((n, half), F32)
    cos_h = jnp.concatenate([cos, cos, pad + 1.0], axis=1)
    slo_h = jnp.concatenate([-sin, zero, pad], axis=1)
    shi_h = jnp.concatenate([zero, sin, pad], axis=1)
    return tuple(jnp.tile(a, (1, N_HEADS)) for a in (cos_h, slo_h, shi_h))


def _conv_epilogue(y, cg, b, lg, lb):
    y = y + b
    mu = jnp.mean(y, axis=-1, keepdims=True)
    yc = y - mu
    var = jnp.mean(yc * yc, axis=-1, keepdims=True)
    z = _silu(yc * lax.rsqrt(var + EPS) * lg + lb)
    return z * _silu(cg)


def _conv_prompt_body(cur_ref, prev_ref, cg_ref, w_ref, b_ref, lg_ref, lb_ref, o_ref, hist_scr):
    j = pl.program_id(1)
    tc = cur_ref.shape[1]
    hist_scr[0:HIST_PAD, :] = jnp.where(j > 0, prev_ref[0], 0.0)
    hist_scr[HIST_PAD:HIST_PAD + tc, :] = cur_ref[0]
    off = HIST_PAD - (CONV_WIDTH - 1)
    for c in range(tc // CONV_SUB):
        r0 = c * CONV_SUB
        acc = jnp.zeros((CONV_SUB, CONV_CH), F32)
        for tap in range(CONV_WIDTH):
            acc = acc + w_ref[tap:tap + 1, :] * hist_scr[r0 + off + tap:r0 + off + tap + CONV_SUB, :]
        o_ref[0, r0:r0 + CONV_SUB, :] = _conv_epilogue(
            acc, cg_ref[0, r0:r0 + CONV_SUB, :], b_ref[...], lg_ref[...], lb_ref[...])


def _conv_prompt(glu, cg, conv_w, conv_b, ln_g, ln_b):
    b, s, c = glu.shape
    tc = CONV_TILE
    per = tc // HIST_PAD
    cmap = lambda i, j: (0, 0)
    return pl.pallas_call(
        _conv_prompt_body,
        grid=(b, s // tc),
        in_specs=[pl.BlockSpec((1, tc, c), lambda i, j: (i, j, 0)),
                  pl.BlockSpec((1, HIST_PAD, c), lambda i, j: (i, jnp.maximum(j * per - 1, 0), 0)),
                  pl.BlockSpec((1, tc, c), lambda i, j: (i, j, 0)),
                  pl.BlockSpec((CONV_WIDTH, c), cmap),
                  pl.BlockSpec((1, c), cmap),
                  pl.BlockSpec((1, c), cmap),
                  pl.BlockSpec((1, c), cmap)],
        out_specs=pl.BlockSpec((1, tc, c), lambda i, j: (i, j, 0)),
        out_shape=jax.ShapeDtypeStruct((b, s, c), F32),
        scratch_shapes=[pltpu.VMEM((HIST_PAD + tc, c), F32)],
        compiler_params=pltpu.CompilerParams(dimension_semantics=("parallel", "parallel"),
                                             vmem_limit_bytes=VMEM_LIMIT),
        name="conv_prompt",
    )(glu, glu, cg, conv_w, conv_b, ln_g, ln_b)


def _conv_sample_body(hist_ref, cg_ref, w_ref, b_ref, lg_ref, lb_ref, o_ref):
    nb, t, c = cg_ref.shape
    acc = jnp.zeros((nb, t, c), F32)
    for tap in range(CONV_WIDTH):
        acc = acc + w_ref[tap:tap + 1, :] * hist_ref[:, tap:tap + t, :]
    o_ref[...] = _conv_epilogue(acc, cg_ref[...], b_ref[...], lg_ref[...], lb_ref[...])


def _conv_sample(hist, cg, conv_w, conv_b, ln_g, ln_b):
    n, t, c = cg.shape
    nb = 8
    cmap = lambda i: (0, 0)
    return pl.pallas_call(
        _conv_sample_body,
        grid=(n // nb,),
        in_specs=[pl.BlockSpec((nb, hist.shape[1], c), lambda i: (i, 0, 0)),
                  pl.BlockSpec((nb, t, c), lambda i: (i, 0, 0)),
                  pl.BlockSpec((CONV_WIDTH, c), cmap),
                  pl.BlockSpec((1, c), cmap),
                  pl.BlockSpec((1, c), cmap),
                  pl.BlockSpec((1, c), cmap)],
        out_specs=pl.BlockSpec((nb, t, c), lambda i: (i, 0, 0)),
        out_shape=jax.ShapeDtypeStruct((n, t, c), F32),
        compiler_params=pltpu.CompilerParams(dimension_semantics=("parallel",),
                                             vmem_limit_bytes=VMEM_LIMIT),
        name="conv_sample",
    )(hist, cg, conv_w, conv_b, ln_g, ln_b)


def _moba_prompt_body(q_ref, k_ref, v_ref, o_ref, kb_scr, vt_scr, bias_scr):
    i = pl.program_id(2)
    tq = q_ref.shape[1]
    nblk = k_ref.shape[1] // MOBA_BLOCK
    heads = q_ref.shape[2] // HEAD_DIM

    @pl.when(i == 0)
    def _():
        for hh in range(heads):
            kb_scr[hh] = k_ref[0, :, hh * HEAD_DIM:(hh + 1) * HEAD_DIM].astype(BF16)
        for blk in range(nblk):
            vt = v_ref[0, blk * MOBA_BLOCK:(blk + 1) * MOBA_BLOCK, :].T.astype(BF16)
            for hh in range(heads):
                vt_scr[hh, blk] = vt[hh * HEAD_DIM:(hh + 1) * HEAD_DIM, :]

    kmean = jnp.mean(k_ref[0].reshape(nblk, MOBA_BLOCK, heads * HEAD_DIM), axis=1)
    blk_iota = lax.broadcasted_iota(jnp.int32, (nblk, tq), 0)
    past = blk_iota < i
    key_iota = lax.broadcasted_iota(jnp.int32, (MOBA_BLOCK, tq), 0)
    qry_iota = lax.broadcasted_iota(jnp.int32, (MOBA_BLOCK, tq), 1)
    outs = []
    for hh in range(heads):
        q_h = q_ref[0, :, hh * HEAD_DIM:(hh + 1) * HEAD_DIM]
        s_blk = _dot_nt(kmean[:, hh * HEAD_DIM:(hh + 1) * HEAD_DIM], q_h,
                        precision=lax.Precision.HIGHEST)
        s_blk = jnp.where(past, s_blk, -jnp.inf)
        rank = jnp.zeros((nblk, tq), jnp.int32)
        for mm in range(nblk):
            sm = s_blk[mm:mm + 1, :]
            beats = jnp.where(sm > s_blk, 1, jnp.where((sm == s_blk) & (blk_iota > mm), 1, 0))
            rank = rank + beats
        bias_scr[hh] = jnp.where(past & (rank < MOBA_TOPK), 0.0, NEG)
        qb = (q_h * (HEAD_DIM ** -0.5)).astype(BF16)

        def attend(j, carry, s_fix):
            m_run, l_run, acc = carry
            kj = kb_scr[hh, pl.ds(pl.multiple_of(j * MOBA_BLOCK, MOBA_BLOCK), MOBA_BLOCK), :]
            s_t = s_fix(_dot_nt(kj, qb))
            m_new = jnp.maximum(m_run, jnp.max(s_t, axis=0, keepdims=True))
            alpha = jnp.exp(m_run - m_new)
            p = jnp.exp(s_t - m_new)
            l_new = alpha * l_run + jnp.sum(p, axis=0, keepdims=True)
            acc_new = alpha * acc + _dot(vt_scr[hh, j], p.astype(BF16))
            return m_new, l_new, acc_new

        def past_block(j, carry):
            return attend(j, carry, lambda s_t: s_t + bias_scr[hh, pl.ds(j, 1), :])

        init = (jnp.full((1, tq), NEG, F32), jnp.zeros((1, tq), F32), jnp.zeros((HEAD_DIM, tq), F32))
        carry = lax.fori_loop(0, i, past_block, init)
        _, l_fin, acc = attend(i, carry, lambda s_t: jnp.where(key_iota <= qry_iota, s_t, NEG))
        outs.append(acc / l_fin)
    o_ref[0] = jnp.concatenate(outs, axis=0).T


def _moba_prompt(q, k, v):
    b, s, w = q.shape
    tq = MOBA_BLOCK
    lanes = 128
    heads = lanes // HEAD_DIM
    return pl.pallas_call(
        _moba_prompt_body,
        grid=(b, w // lanes, s // tq),
        in_specs=[pl.BlockSpec((1, tq, lanes), lambda bi, hp, i: (bi, i, hp)),
                  pl.BlockSpec((1, s, lanes), lambda bi, hp, i: (bi, 0, hp)),
                  pl.BlockSpec((1, s, lanes), lambda bi, hp, i: (bi, 0, hp))],
        out_specs=pl.BlockSpec((1, tq, lanes), lambda bi, hp, i: (bi, i, hp)),
        out_shape=jax.ShapeDtypeStruct((b, s, w), F32),
        scratch_shapes=[pltpu.VMEM((heads, s, HEAD_DIM), BF16),
                        pltpu.VMEM((heads, s // MOBA_BLOCK, HEAD_DIM, MOBA_BLOCK), BF16),
                        pltpu.VMEM((heads, s // MOBA_BLOCK, tq), F32)],
        compiler_params=pltpu.CompilerParams(dimension_semantics=("parallel", "parallel", "arbitrary"),
                                             vmem_limit_bytes=VMEM_LIMIT),
        name="moba_prompt",
    )(q, k, v)


def _moba_sample_body(pt_ref, q_ref, kn_ref, vn_ref, *rest, n_steps, pages_per_step):
    k_pages = rest[:pages_per_step]
    v_pages = rest[pages_per_step:2 * pages_per_step]
    o_ref = rest[2 * pages_per_step]
    qbd_scr, s_scr, mx_scr, sb_scr, p_scr, pown_scr, l_scr, acc_scr = rest[2 * pages_per_step + 1:]
    del pt_ref
    s = pl.program_id(1)
    ds = q_ref.shape[1]
    rows = N_HEADS * ds
    ppb = MOBA_BLOCK // PAGE_SIZE
    n_blocks = n_steps * pages_per_step // ppb
    lane = lax.broadcasted_iota(jnp.int32, (rows, 128), 1)
    row = lax.broadcasted_iota(jnp.int32, (rows, 128), 0)

    @pl.when(s == 0)
    def _():
        q = q_ref[0]
        r_i = lax.broadcasted_iota(jnp.int32, (rows, ATTN_WIDTH), 0)
        l_i = lax.broadcasted_iota(jnp.int32, (rows, ATTN_WIDTH), 1)
        qbd_scr[...] = jnp.where(r_i // ds == l_i // HEAD_DIM, jnp.concatenate([q] * N_HEADS, axis=0), 0.0)
        mx_scr[...] = jnp.full(mx_scr.shape, NEG, F32)
        sb_scr[...] = jnp.zeros(sb_scr.shape, F32)

    @pl.when(s < n_steps)
    def _():
        qb = (qbd_scr[...] * (HEAD_DIM ** -0.5)).astype(BF16)
        mx = mx_scr[...]
        sb = sb_scr[...]
        for bb in range(pages_per_step // ppb):
            blk = s * (pages_per_step // ppb) + bb
            rmax = jnp.full((rows, 1), NEG, F32)
            rsum = jnp.zeros((rows, 1), F32)
            for hp in range(ppb):
                page = k_pages[bb * ppb + hp][0].reshape(ATTN_WIDTH, PAGE_SIZE)
                sc = _dot(qb, page.astype(BF16))
                s_scr[blk * ppb + hp] = sc
                rmax = jnp.maximum(rmax, jnp.max(sc, axis=1, keepdims=True))
                rsum = rsum + jnp.sum(sc, axis=1, keepdims=True)
            mx = jnp.where(lane == blk, rmax, mx)
            sb = jnp.where(lane == blk, rsum, sb)
        mx_scr[...] = mx
        sb_scr[...] = sb

    @pl.when(s == n_steps)
    def _():
        sc = jnp.where(lane < n_blocks, sb_scr[...], -jnp.inf)
        lane_f = lane.astype(F32)
        sel = jnp.zeros((rows, 128), F32)
        for _ in range(min(MOBA_TOPK, n_blocks)):
            top = jnp.max(sc, axis=1, keepdims=True)
            first = jnp.min(jnp.where(sc == top, lane_f, 1e9), axis=1, keepdims=True)
            pick = lane_f == first
            sel = jnp.where(pick, 1.0, sel)
            sc = jnp.where(pick, -jnp.inf, sc)
        kn = jnp.concatenate([kn_ref[0], jnp.zeros((128 - ds, ATTN_WIDTH), F32)], axis=0)
        qb = (qbd_scr[...] * (HEAD_DIM ** -0.5)).astype(BF16)
        s_own = jnp.where(lane <= row % ds, _dot_nt(qb, kn.astype(BF16)), NEG)
        m_sel = jnp.max(jnp.where(sel > 0, mx_scr[...], NEG), axis=1, keepdims=True)
        m_fin = jnp.maximum(m_sel, jnp.max(s_own, axis=1, keepdims=True))
        p_own = jnp.exp(s_own - m_fin)
        pown_scr[...] = p_own

        def block(blk, l_run):
            chosen = jnp.max(jnp.where(lane == blk, sel, 0.0), axis=1, keepdims=True) > 0
            for hp in range(ppb):
                p = jnp.where(chosen, jnp.exp(s_scr[blk * ppb + hp] - m_fin), 0.0)
                p_scr[blk * ppb + hp] = p.astype(BF16)
                l_run = l_run + jnp.sum(p, axis=1, keepdims=True)
            return l_run

        l_fin = lax.fori_loop(0, n_blocks, block, jnp.sum(p_own, axis=1, keepdims=True))
        l_scr[...] = jnp.broadcast_to(l_fin, l_scr.shape)
        acc_scr[...] = jnp.zeros(acc_scr.shape, F32)

    @pl.when(s >= n_steps)
    def _():
        acc = acc_scr[...]
        for pg in range(pages_per_step):
            p = p_scr[(s - n_steps) * pages_per_step + pg]
            vt = v_pages[pg][0].reshape(ATTN_WIDTH, PAGE_SIZE).astype(BF16)
            acc = acc + _dot_nt(p, vt)
        acc_scr[...] = acc

    @pl.when(s == 2 * n_steps - 1)
    def _():
        vn = jnp.concatenate([vn_ref[0], jnp.zeros((128 - ds, ATTN_WIDTH), F32)], axis=0)
        full = acc_scr[...] + _dot(pown_scr[...].astype(BF16), vn.astype(BF16))
        full = full / l_scr[:, 0:1]
        l_i = lax.broadcasted_iota(jnp.int32, (ds, ATTN_WIDTH), 1)
        out = jnp.zeros((ds, ATTN_WIDTH), F32)
        for hh in range(N_HEADS):
            out = out + jnp.where(l_i // HEAD_DIM == hh, full[hh * ds:(hh + 1) * ds, :], 0.0)
        o_ref[0] = out


def _moba_sample(q, k_new, v_new, cache_k, cache_v, page_table):
    db, ds, w = q.shape
    n_pages = page_table.shape[1]
    assert (n_pages * PAGE_SIZE) % MOBA_BLOCK == 0, "past length must be whole MoBA blocks"
    ppb = MOBA_BLOCK // PAGE_SIZE
    pps = SAMPLE_BLOCKS_PER_STEP * ppb
    n_steps = n_pages // pps
    assert n_steps * pps == n_pages
    rows = N_HEADS * ds
    n_phys = cache_k.shape[0]
    ck = jnp.transpose(cache_k, (0, 2, 3, 1))
    cv = jnp.transpose(cache_v, (0, 2, 3, 1))
    pt = page_table.reshape(-1)
    page_block = (1, N_HEADS, HEAD_DIM, PAGE_SIZE)

    def k_map(p):
        return lambda b, s, pt_ref: (pt_ref[b * n_pages + jnp.minimum(s, n_steps - 1) * pps + p], 0, 0, 0)

    def v_map(p):
        return lambda b, s, pt_ref: (pt_ref[b * n_pages + jnp.maximum(s - n_steps, 0) * pps + p], 0, 0, 0)

    tok = pl.BlockSpec((1, ds, w), lambda b, s, pt_ref: (b, 0, 0))
    grid_spec = pltpu.PrefetchScalarGridSpec(
        num_scalar_prefetch=1,
        grid=(db, 2 * n_steps),
        in_specs=[tok, tok, tok]
        + [pl.BlockSpec(page_block, k_map(p)) for p in range(pps)]
        + [pl.BlockSpec(page_block, v_map(p)) for p in range(pps)],
        out_specs=tok,
        scratch_shapes=[pltpu.VMEM((rows, w), F32),
                        pltpu.VMEM((n_pages, rows, PAGE_SIZE), F32),
                        pltpu.VMEM((rows, 128), F32),
                        pltpu.VMEM((rows, 128), F32),
                        pltpu.VMEM((n_pages, rows, PAGE_SIZE), BF16),
                        pltpu.VMEM((rows, 128), F32),
                        pltpu.VMEM((rows, 128), F32),
                        pltpu.VMEM((rows, w), F32)])
    body = functools.partial(_moba_sample_body, n_steps=n_steps, pages_per_step=pps)
    return pl.pallas_call(
        body,
        grid_spec=grid_spec,
        out_shape=jax.ShapeDtypeStruct((db, ds, w), F32),
        compiler_params=pltpu.CompilerParams(dimension_semantics=("parallel", "arbitrary"),
                                             vmem_limit_bytes=VMEM_LIMIT),
        name="moba_sample",
    )(pt, q, k_new, v_new, *([ck] * pps), *([cv] * pps))


def _mixer_out_body(x_ref, yc_ref, ya_ref, ag_ref, gmc_ref, gma_ref, gate_ref, wb_ref, wo_ref, fg_ref, o_ref):
    gb, tb, d = x_ref.shape
    m = gb * tb
    yc = yc_ref[...].reshape(m, CONV_CH).astype(BF16)
    ya = (ya_ref[...] * _silu(ag_ref[...])).reshape(m, ATTN_WIDTH).astype(BF16)
    merged = (jax.nn.sigmoid(gmc_ref[...].reshape(m, d)) * _dot(yc, wb_ref[0])
              + jax.nn.sigmoid(gma_ref[...].reshape(m, d)) * _dot(ya, wb_ref[1]))
    o = _dot(merged.astype(BF16), wo_ref[...]).reshape(gb, tb, d)
    xo = x_ref[...] + gate_ref[...] * o
    r = xo * lax.rsqrt(jnp.mean(xo * xo, axis=-1, keepdims=True) + EPS)
    o_ref[...] = r * fg_ref[...]


def _mixer_out(x3, y_conv, y_attn, ag, gmc, gma, gate, wb_bf, wo_bf, final_g, gb, tb):
    g, t, d = x3.shape
    xmap = lambda i, j: (i, j, 0)
    bmap = lambda i, j: (i, 0, 0)
    spec = lambda w: pl.BlockSpec((gb, tb, w), xmap)
    return pl.pallas_call(
        _mixer_out_body,
        grid=(g // gb, t // tb),
        in_specs=[spec(d), spec(CONV_CH), spec(ATTN_WIDTH), spec(ATTN_WIDTH), spec(d), spec(d),
                  pl.BlockSpec((gb, 1, d), bmap),
                  pl.BlockSpec((2, CONV_CH, d), lambda i, j: (0, 0, 0)),
                  pl.BlockSpec((d, d), lambda i, j: (0, 0)),
                  pl.BlockSpec((1, d), lambda i, j: (0, 0))],
        out_specs=spec(d),
        out_shape=jax.ShapeDtypeStruct((g, t, d), F32),
        compiler_params=pltpu.CompilerParams(dimension_semantics=("parallel", "parallel"),
                                             vmem_limit_bytes=VMEM_LIMIT),
        name="mixer_out",
    )(x3, y_conv, y_attn, ag, gmc, gma, gate, wb_bf, wo_bf, final_g)


def kernel(x_prompt, x_sample, cache_k, cache_v, state_conv, page_table, c_prompt, c_sample, norm_g, w_ada,
           b_ada, w_in, conv_w, conv_b, conv_ln_g, conv_ln_b, w_branch, w_out, final_g):
    depth = norm_g.shape[0]
    assert depth == 1, "single-layer trunk"
    b, s, d = x_prompt.shape
    db, ds, _ = x_sample.shape
    assert CONV_CH == ATTN_WIDTH and s % MOBA_BLOCK == 0 and s >= CONV_WIDTH - 1
    past_len = page_table.shape[1] * PAGE_SIZE
    lyr = 0

    ada = _ada(jnp.concatenate([c_prompt, c_sample], axis=0), w_ada[lyr], b_ada[lyr])
    shift, scale, gate = (ada[:, j * d:(j + 1) * d][:, None, :] for j in range(3))
    w_in_bf = w_in[lyr].astype(BF16)
    wb_bf = w_branch[lyr].astype(BF16)
    wo_bf = w_out[lyr].astype(BF16)
    g_in = norm_g[lyr].reshape(1, d)
    g_fin = final_g.reshape(1, d)
    cw, cb = conv_w[lyr], conv_b[lyr].reshape(1, CONV_CH)
    lg, lb = conv_ln_g[lyr].reshape(1, CONV_CH), conv_ln_b[lyr].reshape(1, CONV_CH)

    tab_p = _rope_tables(jnp.arange(s))
    glu, cg, q, k, v, ag, gmc, gma = _mixer_in(x_prompt, scale[:b], shift[:b], g_in, w_in_bf, tab_p, 1, ROW_TILE)
    y_conv = _conv_prompt(glu, cg, cw, cb, lg, lb)
    y_attn = _moba_prompt(q, k, v)
    y_prompt = _mixer_out(x_prompt, y_conv, y_attn, ag, gmc, gma, gate[:b], wb_bf, wo_bf, g_fin, 1, ROW_TILE)
    k_prompt = k.reshape(1, b, s // PAGE_SIZE, PAGE_SIZE, N_HEADS, HEAD_DIM)
    v_prompt = v.reshape(1, b, s // PAGE_SIZE, PAGE_SIZE, N_HEADS, HEAD_DIM)
    conv_prompt = glu[None, :, s - (CONV_WIDTH - 1):, :]

    gbs = ROW_TILE // ds
    tab_s = tuple(jnp.tile(a, (gbs, 1)) for a in _rope_tables(past_len + jnp.arange(ds)))
    glu, cg, q, k, v, ag, gmc, gma = _mixer_in(x_sample, scale[b:], shift[b:], g_in, w_in_bf, tab_s, gbs, ds)
    hist = jnp.concatenate([state_conv[lyr], glu], axis=1)
    y_conv = _conv_sample(hist, cg, cw, cb, lg, lb)
    y_attn = _moba_sample(q, k, v, cache_k[lyr], cache_v[lyr], page_table)
    y_sample = _mixer_out(x_sample, y_conv, y_attn, ag, gmc, gma, gate[b:], wb_bf, wo_bf, g_fin, gbs, ds)
    k_sample = k.reshape(1, db, ds, N_HEADS, HEAD_DIM)
    v_sample = v.reshape(1, db, ds, N_HEADS, HEAD_DIM)
    conv_sample = hist[None, :, ds:, :]

    return (y_prompt, y_sample, k_prompt, v_prompt, conv_prompt, k_sample, v_sample, conv_sample)
```

```python
import functools

import jax
import jax.numpy as jnp
from jax import lax
from jax.experimental import pallas as pl
from jax.experimental.pallas import tpu as pltpu

F32 = jnp.float32
BF16 = jnp.bfloat16

D_MODEL = 1024
HEAD_DIM = 64
N_HEADS = 8
ATTN_WIDTH = N_HEADS * HEAD_DIM
CONV_CH = 512
ROT_DIM = HEAD_DIM // 4
ROPE_THETA = 500000.0
MOBA_BLOCK = 256
MOBA_TOPK = 3
CONV_WIDTH = 31
PAGE_SIZE = 128
EPS = 1e-6
IN_SIZES = (CONV_CH, CONV_CH, CONV_CH, ATTN_WIDTH, ATTN_WIDTH, ATTN_WIDTH, ATTN_WIDTH, D_MODEL, D_MODEL)
IN_COLS = sum(IN_SIZES)
NEG = -1e30

ROW_TILE = 256
CONV_TILE = 128
CONV_SUB = 32
HIST_PAD = 32
PROMPT_HEADS_PER_STEP = 4
SAMPLE_PAGES_PER_STEP = 8
VMEM_LIMIT = 48 * 1024 * 1024


def _silu(x):
    return x * jax.nn.sigmoid(x)


def _dot(a, b):
    return jnp.dot(a, b, preferred_element_type=F32)


def _dot_nt(a, b, precision=None):
    return lax.dot_general(a, b, (((1,), (1,)), ((), ())), precision=precision,
                           preferred_element_type=F32)


def _ada_body(c_ref, w_ref, b_ref, o_ref):
    a = _silu(c_ref[...]).astype(BF16)
    o_ref[...] = _dot(a, w_ref[...].astype(BF16)) + b_ref[...]


def _ada(c_all, w_ada, b_ada):
    n = c_all.shape[0]
    return pl.pallas_call(
        _ada_body,
        grid=(3,),
        in_specs=[pl.BlockSpec((n, D_MODEL), lambda j: (0, 0)),
                  pl.BlockSpec((D_MODEL, D_MODEL), lambda j: (0, j)),
                  pl.BlockSpec((1, D_MODEL), lambda j: (0, j))],
        out_specs=pl.BlockSpec((n, D_MODEL), lambda j: (0, j)),
        out_shape=jax.ShapeDtypeStruct((n, 3 * D_MODEL), F32),
        compiler_params=pltpu.CompilerParams(dimension_semantics=("arbitrary",),
                                             vmem_limit_bytes=VMEM_LIMIT),
        name="ada",
    )(c_all, w_ada, b_ada.reshape(1, 3 * D_MODEL))


def _rope_rows(x, cos, sin_lo, sin_hi):
    n = x.shape[-1]
    half = ROT_DIM // 2
    return x * cos + pltpu.roll(x, n - half, 1) * sin_lo + pltpu.roll(x, half, 1) * sin_hi


def _mixer_in_body(x_ref, scale_ref, shift_ref, g_ref, w_ref, cos_ref, slo_ref, shi_ref,
                   glu_ref, cg_ref, q_ref, k_ref, v_ref, ag_ref, gmc_ref, gma_ref):
    gb, tb, d = x_ref.shape
    m = gb * tb
    x = x_ref[...]
    r = x * lax.rsqrt(jnp.mean(x * x, axis=-1, keepdims=True) + EPS)
    h = (r * g_ref[...]) * (1.0 + scale_ref[...]) + shift_ref[...]
    hb = h.reshape(m, d).astype(BF16)

    def seg(idx):
        lo = sum(IN_SIZES[:idx])
        return _dot(hb, w_ref[:, lo:lo + IN_SIZES[idx]])

    def put(ref, val):
        ref[...] = val.reshape(ref.shape)

    put(glu_ref, seg(0) * jax.nn.sigmoid(seg(1)))
    put(cg_ref, seg(2))
    cos, slo, shi = cos_ref[...], slo_ref[...], shi_ref[...]
    put(q_ref, _rope_rows(seg(3), cos, slo, shi))
    put(k_ref, _rope_rows(seg(4), cos, slo, shi))
    put(v_ref, seg(5))
    put(ag_ref, seg(6))
    put(gmc_ref, seg(7))
    put(gma_ref, seg(8))


def _mixer_in(x3, scale, shift, norm_g, w_in_bf, tables, gb, tb):
    g, t, d = x3.shape
    m = gb * tb
    grid = (g // gb, t // tb)
    xmap = lambda i, j: (i, j, 0)
    bmap = lambda i, j: (i, 0, 0)
    cmap = lambda i, j: (0, 0)
    tmap = lambda i, j: (j, 0)
    widths = (CONV_CH, CONV_CH, ATTN_WIDTH, ATTN_WIDTH, ATTN_WIDTH, ATTN_WIDTH, D_MODEL, D_MODEL)
    return pl.pallas_call(
        _mixer_in_body,
        grid=grid,
        in_specs=[pl.BlockSpec((gb, tb, d), xmap),
                  pl.BlockSpec((gb, 1, d), bmap),
                  pl.BlockSpec((gb, 1, d), bmap),
                  pl.BlockSpec((1, d), cmap),
                  pl.BlockSpec((d, IN_COLS), cmap, pipeline_mode=pl.Buffered(1)),
                  pl.BlockSpec((m, ATTN_WIDTH), tmap),
                  pl.BlockSpec((m, ATTN_WIDTH), tmap),
                  pl.BlockSpec((m, ATTN_WIDTH), tmap)],
        out_specs=[pl.BlockSpec((gb, tb, w), xmap) for w in widths],
        out_shape=[jax.ShapeDtypeStruct((g, t, w), F32) for w in widths],
        compiler_params=pltpu.CompilerParams(dimension_semantics=("parallel", "parallel"),
                                             vmem_limit_bytes=VMEM_LIMIT),
        name="mixer_in",
    )(x3, scale, shift, norm_g, w_in_bf, *tables)


def _rope_tables(pos):
    inv = ROPE_THETA ** (-(jnp.arange(0, ROT_DIM, 2, dtype=F32) / ROT_DIM))
    ang = pos.astype(F32)[:, None] * inv[None, :]
    cos, sin = jnp.cos(ang), jnp.sin(ang)
    half = ROT_DIM // 2
    n = pos.shape[0]
    pad = jnp.zeros((n, HEAD_DIM - ROT_DIM), F32)
    zero = jnp.zeros((n, half), F32)
    cos_h = jnp.concatenate([cos, cos, pad + 1.0], axis=1)
    slo_h = jnp.concatenate([-sin, zero, pad], axis=1)
    shi_h = jnp.concatenate([zero, sin, pad], axis=1)
    return tuple(jnp.tile(a, (1, N_HEADS)) for a in (cos_h, slo_h, shi_h))


def _conv_epilogue(y, cg, b, lg, lb):
    y = y + b
    mu = jnp.mean(y, axis=-1, keepdims=True)
    yc = y - mu
    var = jnp.mean(yc * yc, axis=-1, keepdims=True)
    z = _silu(yc * lax.rsqrt(var + EPS) * lg + lb)
    return z * _silu(cg)


def _conv_prompt_body(cur_ref, prev_ref, cg_ref, w_ref, b_ref, lg_ref, lb_ref, o_ref, hist_scr):
    j = pl.program_id(1)
    tc = cur_ref.shape[1]
    hist_scr[0:HIST_PAD, :] = jnp.where(j > 0, prev_ref[0], 0.0)
    hist_scr[HIST_PAD:HIST_PAD + tc, :] = cur_ref[0]
    off = HIST_PAD - (CONV_WIDTH - 1)
    for c in range(tc // CONV_SUB):
        r0 = c * CONV_SUB
        acc = jnp.zeros((CONV_SUB, CONV_CH), F32)
        for tap in range(CONV_WIDTH):
            acc = acc + w_ref[tap:tap + 1, :] * hist_scr[r0 + off + tap:r0 + off + tap + CONV_SUB, :]
        o_ref[0, r0:r0 + CONV_SUB, :] = _conv_epilogue(
            acc, cg_ref[0, r0:r0 + CONV_SUB, :], b_ref[...], lg_ref[...], lb_ref[...])


def _conv_prompt(glu, cg, conv_w, conv_b, ln_g, ln_b):
    b, s, c = glu.shape
    tc = CONV_TILE
    per = tc // HIST_PAD
    cmap = lambda i, j: (0, 0)
    return pl.pallas_call(
        _conv_prompt_body,
        grid=(b, s // tc),
        in_specs=[pl.BlockSpec((1, tc, c), lambda i, j: (i, j, 0)),
                  pl.BlockSpec((1, HIST_PAD, c), lambda i, j: (i, jnp.maximum(j * per - 1, 0), 0)),
                  pl.BlockSpec((1, tc, c), lambda i, j: (i, j, 0)),
                  pl.BlockSpec((CONV_WIDTH, c), cmap),
                  pl.BlockSpec((1, c), cmap),
                  pl.BlockSpec((1, c), cmap),
                  pl.BlockSpec((1, c), cmap)],
        out_specs=pl.BlockSpec((1, tc, c), lambda i, j: (i, j, 0)),
        out_shape=jax.ShapeDtypeStruct((b, s, c), F32),
        scratch_shapes=[pltpu.VMEM((HIST_PAD + tc, c), F32)],
        compiler_params=pltpu.CompilerParams(dimension_semantics=("parallel", "parallel"),
                                             vmem_limit_bytes=VMEM_LIMIT),
        name="conv_prompt",
    )(glu, glu, cg, conv_w, conv_b, ln_g, ln_b)


def _conv_sample_body(hist_ref, cg_ref, w_ref, b_ref, lg_ref, lb_ref, o_ref):
    nb, t, c = cg_ref.shape
    acc = jnp.zeros((nb, t, c), F32)
    for tap in range(CONV_WIDTH):
        acc = acc + w_ref[tap:tap + 1, :] * hist_ref[:, tap:tap + t, :]
    o_ref[...] = _conv_epilogue(acc, cg_ref[...], b_ref[...], lg_ref[...], lb_ref[...])


def _conv_sample(hist, cg, conv_w, conv_b, ln_g, ln_b):
    n, t, c = cg.shape
    nb = 8
    cmap = lambda i: (0, 0)
    return pl.pallas_call(
        _conv_sample_body,
        grid=(n // nb,),
        in_specs=[pl.BlockSpec((nb, hist.shape[1], c), lambda i: (i, 0, 0)),
                  pl.BlockSpec((nb, t, c), lambda i: (i, 0, 0)),
                  pl.BlockSpec((CONV_WIDTH, c), cmap),
                  pl.BlockSpec((1, c), cmap),
                  pl.BlockSpec((1, c), cmap),
                  pl.BlockSpec((1, c), cmap)],
        out_specs=pl.BlockSpec((nb, t, c), lambda i: (i, 0, 0)),
        out_shape=jax.ShapeDtypeStruct((n, t, c), F32),
        compiler_params=pltpu.CompilerParams(dimension_semantics=("parallel",),
                                             vmem_limit_bytes=VMEM_LIMIT),
        name="conv_sample",
    )(hist, cg, conv_w, conv_b, ln_g, ln_b)


def _moba_prompt_body(q_ref, k_ref, v_ref, o_ref, kb_scr, vt_scr, bias_scr, s_scr, acc_scr):
    i = pl.program_id(2)
    tq = q_ref.shape[1]
    nblk = k_ref.shape[1] // MOBA_BLOCK
    heads = q_ref.shape[2] // HEAD_DIM

    @pl.when(i == 0)
    def _():
        for hh in range(heads):
            kb_scr[hh] = k_ref[0, :, hh * HEAD_DIM:(hh + 1) * HEAD_DIM].astype(BF16)
        for blk in range(nblk):
            vt = v_ref[0, blk * MOBA_BLOCK:(blk + 1) * MOBA_BLOCK, :].T.astype(BF16)
            for hh in range(heads):
                vt_scr[hh, blk] = vt[hh * HEAD_DIM:(hh + 1) * HEAD_DIM, :]

    kmean = jnp.mean(k_ref[0].reshape(nblk, MOBA_BLOCK, heads * HEAD_DIM), axis=1)
    blk_iota = lax.broadcasted_iota(jnp.int32, (nblk, tq), 0)
    past = blk_iota < i
    causal = (lax.broadcasted_iota(jnp.int32, (MOBA_BLOCK, tq), 0)
              <= lax.broadcasted_iota(jnp.int32, (MOBA_BLOCK, tq), 1))
    qbs = []
    for hh in range(heads):
        q_h = q_ref[0, :, hh * HEAD_DIM:(hh + 1) * HEAD_DIM]
        s_blk = _dot_nt(kmean[:, hh * HEAD_DIM:(hh + 1) * HEAD_DIM], q_h,
                        precision=lax.Precision.HIGHEST)
        s_blk = jnp.where(past, s_blk, -jnp.inf)
        rank = jnp.zeros((nblk, tq), jnp.int32)
        for mm in range(nblk):
            sm = s_blk[mm:mm + 1, :]
            rank = rank + jnp.where(sm > s_blk, 1, jnp.where((sm == s_blk) & (blk_iota > mm), 1, 0))
        bias_scr[hh] = jnp.where(past & (rank < MOBA_TOPK), 0.0, NEG)
        qbs.append((q_h * (HEAD_DIM ** -0.5)).astype(BF16))

    def scores(hh, j):
        kj = kb_scr[hh, pl.ds(pl.multiple_of(j * MOBA_BLOCK, MOBA_BLOCK), MOBA_BLOCK), :]
        return _dot_nt(kj, qbs[hh])

    def fold(x, op):
        return op(x.reshape(MOBA_BLOCK // 8, 8, tq), axis=0)

    mx_own = []
    for hh in range(heads):
        s_t = jnp.where(causal, scores(hh, i), NEG)
        s_scr[hh, i] = s_t
        mx_own.append(fold(s_t, jnp.max))

    def score_block(j, mx):
        out = []
        for hh in range(heads):
            s_t = scores(hh, j) + bias_scr[hh, pl.ds(j, 1), :]
            s_scr[hh, j] = s_t
            out.append(jnp.maximum(mx[hh], fold(s_t, jnp.max)))
        return tuple(out)

    mx = lax.fori_loop(0, i, score_block, tuple(mx_own))
    m_fin = [jnp.max(mx[hh], axis=0, keepdims=True) for hh in range(heads)]

    for hh in range(heads):
        acc_scr[hh] = jnp.zeros((HEAD_DIM, tq), F32)

    def apply_block(j, ls):
        out = []
        for hh in range(heads):
            p = jnp.exp(s_scr[hh, j] - m_fin[hh])
            out.append(ls[hh] + fold(p, jnp.sum))
            acc_scr[hh] = acc_scr[hh] + _dot(vt_scr[hh, j], p.astype(BF16))
        return tuple(out)

    ls = lax.fori_loop(0, i + 1, apply_block, tuple(jnp.zeros((8, tq), F32) for _ in range(heads)))
    o_ref[0] = jnp.concatenate(
        [acc_scr[hh] / jnp.sum(ls[hh], axis=0, keepdims=True) for hh in range(heads)], axis=0).T


def _moba_prompt(q, k, v):
    b, s, w = q.shape
    tq = MOBA_BLOCK
    heads = PROMPT_HEADS_PER_STEP
    lanes = heads * HEAD_DIM
    return pl.pallas_call(
        _moba_prompt_body,
        grid=(b, w // lanes, s // tq),
        in_specs=[pl.BlockSpec((1, tq, lanes), lambda bi, hp, i: (bi, i, hp)),
                  pl.BlockSpec((1, s, lanes), lambda bi, hp, i: (bi, 0, hp)),
                  pl.BlockSpec((1, s, lanes), lambda bi, hp, i: (bi, 0, hp))],
        out_specs=pl.BlockSpec((1, tq, lanes), lambda bi, hp, i: (bi, i, hp)),
        out_shape=jax.ShapeDtypeStruct((b, s, w), F32),
        scratch_shapes=[pltpu.VMEM((heads, s, HEAD_DIM), BF16),
                        pltpu.VMEM((heads, s // MOBA_BLOCK, HEAD_DIM, MOBA_BLOCK), BF16),
                        pltpu.VMEM((heads, s // MOBA_BLOCK, tq), F32),
                        pltpu.VMEM((heads, s // MOBA_BLOCK, MOBA_BLOCK, tq), F32),
                        pltpu.VMEM((heads, HEAD_DIM, tq), F32)],
        compiler_params=pltpu.CompilerParams(dimension_semantics=("parallel", "parallel", "arbitrary"),
                                             vmem_limit_bytes=VMEM_LIMIT),
        name="moba_prompt",
    )(q, k, v)


def _moba_sample_body(pt_ref, q_ref, kn_ref, vn_ref, *rest, n_seq, n_chunks, pages_per_step):
    k_pages = rest[:pages_per_step]
    v_pages = rest[pages_per_step:2 * pages_per_step]
    o_ref = rest[2 * pages_per_step]
    qbd_scr, s_scr, mx_scr, sb_scr, p_scr, pown_scr, l_scr, acc_scr = rest[2 * pages_per_step + 1:]
    del pt_ref
    slot = pl.program_id(0)
    c = pl.program_id(1)
    ds = q_ref.shape[1]
    rows = N_HEADS * ds
    ppb = MOBA_BLOCK // PAGE_SIZE
    n_pages = n_chunks * pages_per_step
    n_blocks = n_pages // ppb
    lane = lax.broadcasted_iota(jnp.int32, (rows, 128), 1)
    row = lax.broadcasted_iota(jnp.int32, (rows, 128), 0)

    def scaled_q():
        return (qbd_scr[...] * (HEAD_DIM ** -0.5)).astype(BF16)

    @pl.when((c == 0) & (slot > 0))
    def _():
        sc = jnp.where(lane < n_blocks, sb_scr[...], -jnp.inf)
        lane_f = lane.astype(F32)
        sel = jnp.zeros((rows, 128), F32)
        for _ in range(min(MOBA_TOPK, n_blocks)):
            top = jnp.max(sc, axis=1, keepdims=True)
            first = jnp.min(jnp.where(sc == top, lane_f, 1e9), axis=1, keepdims=True)
            pick = lane_f == first
            sel = jnp.where(pick, 1.0, sel)
            sc = jnp.where(pick, -jnp.inf, sc)
        kn = jnp.concatenate([kn_ref[0], jnp.zeros((128 - ds, ATTN_WIDTH), F32)], axis=0)
        s_own = jnp.where(lane <= row % ds, _dot_nt(scaled_q(), kn.astype(BF16)), NEG)
        m_sel = jnp.max(jnp.where(sel > 0, mx_scr[...], NEG), axis=1, keepdims=True)
        m_fin = jnp.maximum(m_sel, jnp.max(s_own, axis=1, keepdims=True))
        p_own = jnp.exp(s_own - m_fin)
        pown_scr[...] = p_own
        p_sum = p_own
        for blk in range(n_blocks):
            chosen = sel[:, blk:blk + 1] > 0
            for hp in range(ppb):
                p = jnp.where(chosen, jnp.exp(s_scr[blk * ppb + hp] - m_fin), 0.0)
                p_scr[blk * ppb + hp] = p.astype(BF16)
                p_sum = p_sum + p
        l_scr[...] = jnp.broadcast_to(jnp.sum(p_sum, axis=1, keepdims=True), l_scr.shape)
        acc_scr[...] = jnp.zeros(acc_scr.shape, F32)

    @pl.when((c == 0) & (slot < n_seq))
    def _():
        q = q_ref[0]
        r_i = lax.broadcasted_iota(jnp.int32, (rows, ATTN_WIDTH), 0)
        l_i = lax.broadcasted_iota(jnp.int32, (rows, ATTN_WIDTH), 1)
        qbd_scr[...] = jnp.where(r_i // ds == l_i // HEAD_DIM, jnp.concatenate([q] * N_HEADS, axis=0), 0.0)
        mx_scr[...] = jnp.full(mx_scr.shape, NEG, F32)
        sb_scr[...] = jnp.zeros(sb_scr.shape, F32)

    def score_k_pages():
        qb = scaled_q()
        mx = mx_scr[...]
        sb = sb_scr[...]
        for bb in range(pages_per_step // ppb):
            blk = c * (pages_per_step // ppb) + bb
            rmax = jnp.full((rows, 1), NEG, F32)
            rsum = jnp.zeros((rows, 1), F32)
            for hp in range(ppb):
                page = k_pages[bb * ppb + hp][0].reshape(ATTN_WIDTH, PAGE_SIZE)
                sc = _dot(qb, page.astype(BF16))
                s_scr[blk * ppb + hp] = sc
                rmax = jnp.maximum(rmax, jnp.max(sc, axis=1, keepdims=True))
                rsum = rsum + jnp.sum(sc, axis=1, keepdims=True)
            mx = jnp.where(lane == blk, rmax, mx)
            sb = jnp.where(lane == blk, rsum, sb)
        mx_scr[...] = mx
        sb_scr[...] = sb

    def apply_v_pages():
        acc = acc_scr[...]
        for pg in range(pages_per_step):
            p = p_scr[c * pages_per_step + pg]
            vt = v_pages[pg][0].reshape(ATTN_WIDTH, PAGE_SIZE).astype(BF16)
            acc = acc + _dot_nt(p, vt)
        acc_scr[...] = acc

    @pl.when((slot > 0) & (slot < n_seq))
    def _():
        score_k_pages()
        apply_v_pages()

    @pl.when(slot == 0)
    def _():
        score_k_pages()

    @pl.when(slot == n_seq)
    def _():
        apply_v_pages()

    @pl.when((slot > 0) & (c == n_chunks - 1))
    def _():
        vn = jnp.concatenate([vn_ref[0], jnp.zeros((128 - ds, ATTN_WIDTH), F32)], axis=0)
        full = acc_scr[...] + _dot(pown_scr[...].astype(BF16), vn.astype(BF16))
        full = full / l_scr[:, 0:1]
        l_i = lax.broadcasted_iota(jnp.int32, (ds, ATTN_WIDTH), 1)
        out = jnp.zeros((ds, ATTN_WIDTH), F32)
        for hh in range(N_HEADS):
            out = out + jnp.where(l_i // HEAD_DIM == hh, full[hh * ds:(hh + 1) * ds, :], 0.0)
        o_ref[0] = out


def _moba_sample(q, k_new, v_new, cache_k, cache_v, page_table):
    db, ds, w = q.shape
    n_pages = page_table.shape[1]
    assert (n_pages * PAGE_SIZE) % MOBA_BLOCK == 0, "past length must be whole MoBA blocks"
    pps = SAMPLE_PAGES_PER_STEP
    n_chunks = n_pages // pps
    assert n_chunks * pps == n_pages and pps % (MOBA_BLOCK // PAGE_SIZE) == 0
    rows = N_HEADS * ds
    ck = jnp.transpose(cache_k, (0, 2, 3, 1))
    cv = jnp.transpose(cache_v, (0, 2, 3, 1))
    pt = page_table.reshape(-1)
    page_block = (1, N_HEADS, HEAD_DIM, PAGE_SIZE)

    def k_map(p):
        def index(slot, c, pt_ref):
            seq = jnp.minimum(slot, db - 1)
            chunk = jnp.where(slot < db, c, n_chunks - 1)
            return (pt_ref[seq * n_pages + chunk * pps + p], 0, 0, 0)
        return index

    def v_map(p):
        def index(slot, c, pt_ref):
            seq = jnp.maximum(slot - 1, 0)
            chunk = jnp.where(slot > 0, c, 0)
            return (pt_ref[seq * n_pages + chunk * pps + p], 0, 0, 0)
        return index

    cur = pl.BlockSpec((1, ds, w), lambda slot, c, pt_ref: (jnp.minimum(slot, db - 1), 0, 0))
    prev = pl.BlockSpec((1, ds, w), lambda slot, c, pt_ref: (jnp.maximum(slot - 1, 0), 0, 0))
    grid_spec = pltpu.PrefetchScalarGridSpec(
        num_scalar_prefetch=1,
        grid=(db + 1, n_chunks),
        in_specs=[cur, prev, prev]
        + [pl.BlockSpec(page_block, k_map(p)) for p in range(pps)]
        + [pl.BlockSpec(page_block, v_map(p)) for p in range(pps)],
        out_specs=prev,
        scratch_shapes=[pltpu.VMEM((rows, w), F32),
                        pltpu.VMEM((n_pages, rows, PAGE_SIZE), F32),
                        pltpu.VMEM((rows, 128), F32),
                        pltpu.VMEM((rows, 128), F32),
                        pltpu.VMEM((n_pages, rows, PAGE_SIZE), BF16),
                        pltpu.VMEM((rows, 128), F32),
                        pltpu.VMEM((rows, 128), F32),
                        pltpu.VMEM((rows, w), F32)])
    body = functools.partial(_moba_sample_body, n_seq=db, n_chunks=n_chunks, pages_per_step=pps)
    return pl.pallas_call(
        body,
        grid_spec=grid_spec,
        out_shape=jax.ShapeDtypeStruct((db, ds, w), F32),
        compiler_params=pltpu.CompilerParams(dimension_semantics=("arbitrary", "arbitrary"),
                                             vmem_limit_bytes=VMEM_LIMIT),
        name="moba_sample",
    )(pt, q, k_new, v_new, *([ck] * pps), *([cv] * pps))


def _mixer_out_body(x_ref, yc_ref, ya_ref, ag_ref, gmc_ref, gma_ref, gate_ref, wb_ref, wo_ref, fg_ref, o_ref):
    gb, tb, d = x_ref.shape
    m = gb * tb
    yc = yc_ref[...].reshape(m, CONV_CH).astype(BF16)
    ya = (ya_ref[...] * _silu(ag_ref[...])).reshape(m, ATTN_WIDTH).astype(BF16)
    merged = (jax.nn.sigmoid(gmc_ref[...].reshape(m, d)) * _dot(yc, wb_ref[0])
              + jax.nn.sigmoid(gma_ref[...].reshape(m, d)) * _dot(ya, wb_ref[1]))
    o = _dot(merged.astype(BF16), wo_ref[...]).reshape(gb, tb, d)
    xo = x_ref[...] + gate_ref[...] * o
    r = xo * lax.rsqrt(jnp.mean(xo * xo, axis=-1, keepdims=True) + EPS)
    o_ref[...] = r * fg_ref[...]


def _mixer_out(x3, y_conv, y_attn, ag, gmc, gma, gate, wb_bf, wo_bf, final_g, gb, tb):
    g, t, d = x3.shape
    xmap = lambda i, j: (i, j, 0)
    bmap = lambda i, j: (i, 0, 0)
    spec = lambda w: pl.BlockSpec((gb, tb, w), xmap)
    return pl.pallas_call(
        _mixer_out_body,
        grid=(g // gb, t // tb),
        in_specs=[spec(d), spec(CONV_CH), spec(ATTN_WIDTH), spec(ATTN_WIDTH), spec(d), spec(d),
                  pl.BlockSpec((gb, 1, d), bmap),
                  pl.BlockSpec((2, CONV_CH, d), lambda i, j: (0, 0, 0)),
                  pl.BlockSpec((d, d), lambda i, j: (0, 0)),
                  pl.BlockSpec((1, d), lambda i, j: (0, 0))],
        out_specs=spec(d),
        out_shape=jax.ShapeDtypeStruct((g, t, d), F32),
        compiler_params=pltpu.CompilerParams(dimension_semantics=("parallel", "parallel"),
                                             vmem_limit_bytes=VMEM_LIMIT),
        name="mixer_out",
    )(x3, y_conv, y_attn, ag, gmc, gma, gate, wb_bf, wo_bf, final_g)


def kernel(x_prompt, x_sample, cache_k, cache_v, state_conv, page_table, c_prompt, c_sample, norm_g, w_ada,
           b_ada, w_in, conv_w, conv_b, conv_ln_g, conv_ln_b, w_branch, w_out, final_g):
    depth = norm_g.shape[0]
    assert depth == 1, "single-layer trunk"
    b, s, d = x_prompt.shape
    db, ds, _ = x_sample.shape
    assert CONV_CH == ATTN_WIDTH and s % MOBA_BLOCK == 0 and s >= CONV_WIDTH - 1
    past_len = page_table.shape[1] * PAGE_SIZE
    lyr = 0

    ada = _ada(jnp.concatenate([c_prompt, c_sample], axis=0), w_ada[lyr], b_ada[lyr])
    shift, scale, gate = (ada[:, j * d:(j + 1) * d][:, None, :] for j in range(3))
    w_in_bf = w_in[lyr].astype(BF16)
    wb_bf = w_branch[lyr].astype(BF16)
    wo_bf = w_out[lyr].astype(BF16)
    g_in = norm_g[lyr].reshape(1, d)
    g_fin = final_g.reshape(1, d)
    cw, cb = conv_w[lyr], conv_b[lyr].reshape(1, CONV_CH)
    lg, lb = conv_ln_g[lyr].reshape(1, CONV_CH), conv_ln_b[lyr].reshape(1, CONV_CH)

    tab_p = _rope_tables(jnp.arange(s))
    glu, cg, q, k, v, ag, gmc, gma = _mixer_in(x_prompt, scale[:b], shift[:b], g_in, w_in_bf, tab_p, 1, ROW_TILE)
    y_conv = _conv_prompt(glu, cg, cw, cb, lg, lb)
    y_attn = _moba_prompt(q, k, v)
    y_prompt = _mixer_out(x_prompt, y_conv, y_attn, ag, gmc, gma, gate[:b], wb_bf, wo_bf, g_fin, 1, ROW_TILE)
    k_prompt = k.reshape(1, b, s // PAGE_SIZE, PAGE_SIZE, N_HEADS, HEAD_DIM)
    v_prompt = v.reshape(1, b, s // PAGE_SIZE, PAGE_SIZE, N_HEADS, HEAD_DIM)
    conv_prompt = glu[None, :, s - (CONV_WIDTH - 1):, :]

    gbs = ROW_TILE // ds
    tab_s = tuple(jnp.tile(a, (gbs, 1)) for a in _rope_tables(past_len + jnp.arange(ds)))
    glu, cg, q, k, v, ag, gmc, gma = _mixer_in(x_sample, scale[b:], shift[b:], g_in, w_in_bf, tab_s, gbs, ds)
    hist = jnp.concatenate([state_conv[lyr], glu], axis=1)
    y_conv = _conv_sample(hist, cg, cw, cb, lg, lb)
    y_attn = _moba_sample(q, k, v, cache_k[lyr], cache_v[lyr], page_table)
    y_sample = _mixer_out(x_sample, y_conv, y_attn, ag, gmc, gma, gate[b:], wb_bf, wo_bf, g_fin, gbs, ds)
    k_sample = k.reshape(1, db, ds, N_HEADS, HEAD_DIM)
    v_sample = v.reshape(1, db, ds, N_HEADS, HEAD_DIM)
    conv_sample = hist[None, :, ds:, :]

    return (y_prompt, y_sample, k_prompt, v_prompt, conv_prompt, k_sample, v_sample, conv_sample)
```

```python
import functools

import jax
import jax.numpy as jnp
from jax import lax
from jax.experimental import pallas as pl
from jax.experimental.pallas import tpu as pltpu

F32 = jnp.float32
BF16 = jnp.bfloat16

D_MODEL = 1024
HEAD_DIM = 64
N_HEADS = 8
ATTN_WIDTH = N_HEADS * HEAD_DIM
CONV_CH = 512
ROT_DIM = HEAD_DIM // 4
ROPE_THETA = 500000.0
MOBA_BLOCK = 256
MOBA_TOPK = 3
CONV_WIDTH = 31
PAGE_SIZE = 128
EPS = 1e-6
IN_SIZES = (CONV_CH, CONV_CH, CONV_CH, ATTN_WIDTH, ATTN_WIDTH, ATTN_WIDTH, ATTN_WIDTH, D_MODEL, D_MODEL)
IN_COLS = sum(IN_SIZES)
NEG = -1e30

ROW_TILE = 256
CONV_TILE = 256
CONV_SUB = 32
HIST_PAD = 32
PROMPT_HEADS_PER_STEP = 4
SAMPLE_PAGES_PER_STEP = 8
SAMPLE_RING_SLOTS = 3
VMEM_LIMIT = 48 * 1024 * 1024


def _silu(x):
    return x * jax.nn.sigmoid(x)


def _dot(a, b):
    return jnp.dot(a, b, preferred_element_type=F32)


def _dot_nt(a, b, precision=None):
    return lax.dot_general(a, b, (((1,), (1,)), ((), ())), precision=precision,
                           preferred_element_type=F32)


def _ada_body(c_ref, w_ref, b_ref, o_ref):
    a = _silu(c_ref[...]).astype(BF16)
    o_ref[...] = _dot(a, w_ref[...].astype(BF16)) + b_ref[...]


def _ada(c_all, w_ada, b_ada):
    n = c_all.shape[0]
    return pl.pallas_call(
        _ada_body,
        grid=(3,),
        in_specs=[pl.BlockSpec((n, D_MODEL), lambda j: (0, 0)),
                  pl.BlockSpec((D_MODEL, D_MODEL), lambda j: (0, j)),
                  pl.BlockSpec((1, D_MODEL), lambda j: (0, j))],
        out_specs=pl.BlockSpec((n, D_MODEL), lambda j: (0, j)),
        out_shape=jax.ShapeDtypeStruct((n, 3 * D_MODEL), F32),
        compiler_params=pltpu.CompilerParams(dimension_semantics=("arbitrary",),
                                             vmem_limit_bytes=VMEM_LIMIT),
        name="ada",
    )(c_all, w_ada, b_ada.reshape(1, 3 * D_MODEL))


def _rope_rows(x, cos, sin_lo, sin_hi):
    n = x.shape[-1]
    half = ROT_DIM // 2
    return x * cos + pltpu.roll(x, n - half, 1) * sin_lo + pltpu.roll(x, half, 1) * sin_hi


def _mixer_in_body(x_ref, scale_ref, shift_ref, g_ref, w_ref, cos_ref, slo_ref, shi_ref,
                   glu_ref, cg_ref, q_ref, k_ref, v_ref, ag_ref, gmc_ref, gma_ref, *, paged_kv):
    gb, tb, d = x_ref.shape
    m = gb * tb
    x = x_ref[...]
    r = x * lax.rsqrt(jnp.mean(x * x, axis=-1, keepdims=True) + EPS)
    h = (r * g_ref[...]) * (1.0 + scale_ref[...]) + shift_ref[...]
    hb = h.reshape(m, d).astype(BF16)

    def seg(idx):
        lo = sum(IN_SIZES[:idx])
        return _dot(hb, w_ref[:, lo:lo + IN_SIZES[idx]])

    def put(ref, val):
        ref[...] = val.reshape(ref.shape)

    def put_kv(ref, val):
        if not paged_kv:
            return put(ref, val)
        val_t = val.T
        for pg in range(m // PAGE_SIZE):
            ref[0, pg] = val_t[:, pg * PAGE_SIZE:(pg + 1) * PAGE_SIZE].reshape(N_HEADS, HEAD_DIM, PAGE_SIZE)

    put(glu_ref, seg(0) * jax.nn.sigmoid(seg(1)))
    put(cg_ref, seg(2))
    cos, slo, shi = cos_ref[...], slo_ref[...], shi_ref[...]
    put(q_ref, _rope_rows(seg(3), cos, slo, shi))
    put_kv(k_ref, _rope_rows(seg(4), cos, slo, shi))
    put_kv(v_ref, seg(5))
    put(ag_ref, seg(6))
    put(gmc_ref, seg(7))
    put(gma_ref, seg(8))


def _mixer_in(x3, scale, shift, norm_g, w_in_bf, tables, gb, tb, paged_kv):
    g, t, d = x3.shape
    m = gb * tb
    grid = (g // gb, t // tb)
    xmap = lambda i, j: (i, j, 0)
    bmap = lambda i, j: (i, 0, 0)
    cmap = lambda i, j: (0, 0)
    tmap = lambda i, j: (j, 0)
    widths = (CONV_CH, CONV_CH, ATTN_WIDTH, ATTN_WIDTH, ATTN_WIDTH, ATTN_WIDTH, D_MODEL, D_MODEL)
    out_specs = [pl.BlockSpec((gb, tb, w), xmap) for w in widths]
    out_shape = [jax.ShapeDtypeStruct((g, t, w), F32) for w in widths]
    if paged_kv:
        assert gb == 1 and tb % PAGE_SIZE == 0
        for idx in (3, 4):
            out_specs[idx] = pl.BlockSpec((1, tb // PAGE_SIZE, N_HEADS, HEAD_DIM, PAGE_SIZE),
                                          lambda i, j: (i, j, 0, 0, 0))
            out_shape[idx] = jax.ShapeDtypeStruct((g, t // PAGE_SIZE, N_HEADS, HEAD_DIM, PAGE_SIZE), F32)
    return pl.pallas_call(
        functools.partial(_mixer_in_body, paged_kv=paged_kv),
        grid=grid,
        in_specs=[pl.BlockSpec((gb, tb, d), xmap),
                  pl.BlockSpec((gb, 1, d), bmap),
                  pl.BlockSpec((gb, 1, d), bmap),
                  pl.BlockSpec((1, d), cmap),
                  pl.BlockSpec((d, IN_COLS), cmap, pipeline_mode=pl.Buffered(1)),
                  pl.BlockSpec((m, ATTN_WIDTH), tmap),
                  pl.BlockSpec((m, ATTN_WIDTH), tmap),
                  pl.BlockSpec((m, ATTN_WIDTH), tmap)],
        out_specs=out_specs,
        out_shape=out_shape,
        compiler_params=pltpu.CompilerParams(dimension_semantics=("parallel", "parallel"),
                                             vmem_limit_bytes=VMEM_LIMIT),
        name="mixer_in",
    )(x3, scale, shift, norm_g, w_in_bf, *tables)


def _rope_tables(pos):
    inv = ROPE_THETA ** (-(jnp.arange(0, ROT_DIM, 2, dtype=F32) / ROT_DIM))
    ang = pos.astype(F32)[:, None] * inv[None, :]
    cos, sin = jnp.cos(ang), jnp.sin(ang)
    half = ROT_DIM // 2
    n = pos.shape[0]
    pad = jnp.zeros((n, HEAD_DIM - ROT_DIM), F32)
    zero = jnp.zeros((n, half), F32)
    cos_h = jnp.concatenate([cos, cos, pad + 1.0], axis=1)
    slo_h = jnp.concatenate([-sin, zero, pad], axis=1)
    shi_h = jnp.concatenate([zero, sin, pad], axis=1)
    return tuple(jnp.tile(a, (1, N_HEADS)) for a in (cos_h, slo_h, shi_h))


def _conv_epilogue(y, cg, b, lg, lb):
    y = y + b
    mu = jnp.mean(y, axis=-1, keepdims=True)
    yc = y - mu
    var = jnp.mean(yc * yc, axis=-1, keepdims=True)
    z = _silu(yc * lax.rsqrt(var + EPS) * lg + lb)
    return z * _silu(cg)


def _conv_tap_offsets():
    off = HIST_PAD - (CONV_WIDTH - 1)
    return [divmod(off + tap, 8) for tap in range(CONV_WIDTH)]


def _conv_shift_span(tc):
    return tc + 8 * max(a for a, r in _conv_tap_offsets() if r)


def _conv_prompt_body(cur_ref, prev_ref, cg_ref, w_ref, b_ref, lg_ref, lb_ref, o_ref, hist_scr, shift_scr):
    j = pl.program_id(1)
    tc = cur_ref.shape[1]
    hist_scr[0:HIST_PAD, :] = jnp.where(j > 0, prev_ref[0], 0.0)
    hist_scr[HIST_PAD:HIST_PAD + tc, :] = cur_ref[0]
    span = _conv_shift_span(tc)
    for r in range(1, 8):
        shift_scr[r - 1] = hist_scr[r:r + span, :]
    for c in range(tc // CONV_SUB):
        r0 = c * CONV_SUB
        acc = jnp.zeros((CONV_SUB, CONV_CH), F32)
        for tap, (a, r) in enumerate(_conv_tap_offsets()):
            lo = r0 + 8 * a
            win = hist_scr[lo:lo + CONV_SUB, :] if r == 0 else shift_scr[r - 1, lo:lo + CONV_SUB, :]
            acc = acc + w_ref[tap:tap + 1, :] * win
        o_ref[0, r0:r0 + CONV_SUB, :] = _conv_epilogue(
            acc, cg_ref[0, r0:r0 + CONV_SUB, :], b_ref[...], lg_ref[...], lb_ref[...])


def _conv_prompt(glu, cg, conv_w, conv_b, ln_g, ln_b):
    b, s, c = glu.shape
    tc = CONV_TILE
    per = tc // HIST_PAD
    cmap = lambda i, j: (0, 0)
    return pl.pallas_call(
        _conv_prompt_body,
        grid=(b, s // tc),
        in_specs=[pl.BlockSpec((1, tc, c), lambda i, j: (i, j, 0)),
                  pl.BlockSpec((1, HIST_PAD, c), lambda i, j: (i, jnp.maximum(j * per - 1, 0), 0)),
                  pl.BlockSpec((1, tc, c), lambda i, j: (i, j, 0)),
                  pl.BlockSpec((CONV_WIDTH, c), cmap),
                  pl.BlockSpec((1, c), cmap),
                  pl.BlockSpec((1, c), cmap),
                  pl.BlockSpec((1, c), cmap)],
        out_specs=pl.BlockSpec((1, tc, c), lambda i, j: (i, j, 0)),
        out_shape=jax.ShapeDtypeStruct((b, s, c), F32),
        scratch_shapes=[pltpu.VMEM((HIST_PAD + tc, c), F32),
                        pltpu.VMEM((7, _conv_shift_span(tc), c), F32)],
        compiler_params=pltpu.CompilerParams(dimension_semantics=("parallel", "parallel"),
                                             vmem_limit_bytes=VMEM_LIMIT),
        name="conv_prompt",
    )(glu, glu, cg, conv_w, conv_b, ln_g, ln_b)


def _conv_sample_body(hist_ref, cg_ref, w_ref, b_ref, lg_ref, lb_ref, o_ref):
    nb, t, c = cg_ref.shape
    acc = jnp.zeros((nb, t, c), F32)
    for tap in range(CONV_WIDTH):
        acc = acc + w_ref[tap:tap + 1, :] * hist_ref[:, tap:tap + t, :]
    o_ref[...] = _conv_epilogue(acc, cg_ref[...], b_ref[...], lg_ref[...], lb_ref[...])


def _conv_sample(hist, cg, conv_w, conv_b, ln_g, ln_b):
    n, t, c = cg.shape
    nb = 8
    cmap = lambda i: (0, 0)
    return pl.pallas_call(
        _conv_sample_body,
        grid=(n // nb,),
        in_specs=[pl.BlockSpec((nb, hist.shape[1], c), lambda i: (i, 0, 0)),
                  pl.BlockSpec((nb, t, c), lambda i: (i, 0, 0)),
                  pl.BlockSpec((CONV_WIDTH, c), cmap),
                  pl.BlockSpec((1, c), cmap),
                  pl.BlockSpec((1, c), cmap),
                  pl.BlockSpec((1, c), cmap)],
        out_specs=pl.BlockSpec((nb, t, c), lambda i: (i, 0, 0)),
        out_shape=jax.ShapeDtypeStruct((n, t, c), F32),
        compiler_params=pltpu.CompilerParams(dimension_semantics=("parallel",),
                                             vmem_limit_bytes=VMEM_LIMIT),
        name="conv_sample",
    )(hist, cg, conv_w, conv_b, ln_g, ln_b)


def _moba_prompt_body(q_ref, k_ref, v_ref, o_ref, kb_scr, vt_scr, kmean_scr, bias_scr, s_scr, acc_scr):
    i = pl.program_id(2)
    tq = q_ref.shape[1]
    n_pages, heads = k_ref.shape[1], k_ref.shape[2]
    ppb = MOBA_BLOCK // PAGE_SIZE
    nblk = n_pages // ppb

    @pl.when(i == 0)
    def _():
        for blk in range(nblk):
            ksum = jnp.zeros((1, heads * HEAD_DIM), F32)
            for hp in range(ppb):
                pg = blk * ppb + hp
                k_rows = k_ref[0, pg].reshape(heads * HEAD_DIM, PAGE_SIZE).T
                ksum = ksum + jnp.sum(k_rows, axis=0, keepdims=True)
                for hh in range(heads):
                    kb_scr[hh, pg * PAGE_SIZE:(pg + 1) * PAGE_SIZE, :] = (
                        k_rows[:, hh * HEAD_DIM:(hh + 1) * HEAD_DIM].astype(BF16))
            kmean_scr[blk:blk + 1, :] = ksum * (1.0 / MOBA_BLOCK)
            for hh in range(heads):
                vt_scr[hh, blk] = jnp.concatenate(
                    [v_ref[0, blk * ppb + hp, hh] for hp in range(ppb)], axis=1).astype(BF16)

    kmean = kmean_scr[...]
    blk_iota = lax.broadcasted_iota(jnp.int32, (nblk, tq), 0)
    past = blk_iota < i
    causal = (lax.broadcasted_iota(jnp.int32, (MOBA_BLOCK, tq), 0)
              <= lax.broadcasted_iota(jnp.int32, (MOBA_BLOCK, tq), 1))
    qbs = []
    for hh in range(heads):
        q_h = q_ref[0, :, hh * HEAD_DIM:(hh + 1) * HEAD_DIM]
        s_blk = _dot_nt(kmean[:, hh * HEAD_DIM:(hh + 1) * HEAD_DIM], q_h,
                        precision=lax.Precision.HIGHEST)
        s_blk = jnp.where(past, s_blk, -jnp.inf)
        rank = jnp.zeros((nblk, tq), jnp.int32)
        for mm in range(nblk):
            sm = s_blk[mm:mm + 1, :]
            rank = rank + jnp.where(sm > s_blk, 1, jnp.where((sm == s_blk) & (blk_iota > mm), 1, 0))
        bias_scr[hh] = jnp.where(past & (rank < MOBA_TOPK), 0.0, NEG)
        qbs.append((q_h * (HEAD_DIM ** -0.5)).astype(BF16))

    def scores(hh, j):
        kj = kb_scr[hh, pl.ds(pl.multiple_of(j * MOBA_BLOCK, MOBA_BLOCK), MOBA_BLOCK), :]
        return _dot_nt(kj, qbs[hh])

    def fold(x, op):
        return op(x.reshape(MOBA_BLOCK // 8, 8, tq), axis=0)

    mx_own = []
    for hh in range(heads):
        s_t = jnp.where(causal, scores(hh, i), NEG)
        s_scr[hh, i] = s_t
        mx_own.append(fold(s_t, jnp.max))

    def score_block(j, mx):
        out = []
        for hh in range(heads):
            s_t = scores(hh, j) + bias_scr[hh, pl.ds(j, 1), :]
            s_scr[hh, j] = s_t
            out.append(jnp.maximum(mx[hh], fold(s_t, jnp.max)))
        return tuple(out)

    mx = lax.fori_loop(0, i, score_block, tuple(mx_own))
    m_fin = [jnp.max(mx[hh], axis=0, keepdims=True) for hh in range(heads)]

    for hh in range(heads):
        acc_scr[hh] = jnp.zeros((HEAD_DIM, tq), F32)

    def apply_block(j, ls):
        out = []
        for hh in range(heads):
            p = jnp.exp(s_scr[hh, j] - m_fin[hh])
            out.append(ls[hh] + fold(p, jnp.sum))
            acc_scr[hh] = acc_scr[hh] + _dot(vt_scr[hh, j], p.astype(BF16))
        return tuple(out)

    ls = lax.fori_loop(0, i + 1, apply_block, tuple(jnp.zeros((8, tq), F32) for _ in range(heads)))
    o_ref[0] = jnp.concatenate(
        [acc_scr[hh] / jnp.sum(ls[hh], axis=0, keepdims=True) for hh in range(heads)], axis=0).T


def _moba_prompt(q, k_paged, v_paged):
    b, s, w = q.shape
    tq = MOBA_BLOCK
    heads = PROMPT_HEADS_PER_STEP
    lanes = heads * HEAD_DIM
    kv_spec = pl.BlockSpec((1, s // PAGE_SIZE, heads, HEAD_DIM, PAGE_SIZE), lambda bi, hp, i: (bi, 0, hp, 0, 0))
    return pl.pallas_call(
        _moba_prompt_body,
        grid=(b, w // lanes, s // tq),
        in_specs=[pl.BlockSpec((1, tq, lanes), lambda bi, hp, i: (bi, i, hp)), kv_spec, kv_spec],
        out_specs=pl.BlockSpec((1, tq, lanes), lambda bi, hp, i: (bi, i, hp)),
        out_shape=jax.ShapeDtypeStruct((b, s, w), F32),
        scratch_shapes=[pltpu.VMEM((heads, s, HEAD_DIM), BF16),
                        pltpu.VMEM((heads, s // MOBA_BLOCK, HEAD_DIM, MOBA_BLOCK), BF16),
                        pltpu.VMEM((s // MOBA_BLOCK, lanes), F32),
                        pltpu.VMEM((heads, s // MOBA_BLOCK, tq), F32),
                        pltpu.VMEM((heads, s // MOBA_BLOCK, MOBA_BLOCK, tq), F32),
                        pltpu.VMEM((heads, HEAD_DIM, tq), F32)],
        compiler_params=pltpu.CompilerParams(dimension_semantics=("parallel", "parallel", "arbitrary"),
                                             vmem_limit_bytes=VMEM_LIMIT),
        name="moba_prompt",
    )(q, k_paged, v_paged)


def _moba_sample_body(pt_ref, q_ref, kn_ref, vn_ref, ck_hbm, cv_hbm, o_ref,
                      kbuf, vbuf, ksem, vsem, qbd_scr, s_scr, mx_scr, sb_scr, p_scr, pown_scr, l_scr, acc_scr,
                      *, n_seq, n_chunks, pages_per_step):
    slot = pl.program_id(0)
    ds = q_ref.shape[1]
    rows = N_HEADS * ds
    ppb = MOBA_BLOCK // PAGE_SIZE
    n_pages = n_chunks * pages_per_step
    n_blocks = n_pages // ppb
    n_global = (n_seq + 1) * n_chunks
    lookahead = SAMPLE_RING_SLOTS - 1
    lane = lax.broadcasted_iota(jnp.int32, (rows, 128), 1)
    row = lax.broadcasted_iota(jnp.int32, (rows, 128), 0)

    def page_copies(g, start):
        g_slot = g // n_chunks
        g_c = g % n_chunks
        ring = g % SAMPLE_RING_SLOTS

        def each(cache, buf, sem, seq):
            for p in range(pages_per_step):
                if start:
                    page = pt_ref[seq * n_pages + g_c * pages_per_step + p]
                    pltpu.make_async_copy(cache.at[page], buf.at[ring, p], sem.at[ring]).start()
                else:
                    pltpu.make_async_copy(cache.at[0], buf.at[ring, p], sem.at[ring]).wait()

        @pl.when(g_slot < n_seq)
        def _():
            each(ck_hbm, kbuf, ksem, g_slot)

        @pl.when(g_slot > 0)
        def _():
            each(cv_hbm, vbuf, vsem, g_slot - 1)

    @pl.when(slot == 0)
    def _():
        for g in range(lookahead):
            page_copies(jnp.int32(g), start=True)

    def scaled_q():
        return (qbd_scr[...] * (HEAD_DIM ** -0.5)).astype(BF16)

    @pl.when(slot > 0)
    def _():
        sc = jnp.where(lane < n_blocks, sb_scr[...], -jnp.inf)
        lane_f = lane.astype(F32)
        sel = jnp.zeros((rows, 128), F32)
        for _ in range(min(MOBA_TOPK, n_blocks)):
            top = jnp.max(sc, axis=1, keepdims=True)
            first = jnp.min(jnp.where(sc == top, lane_f, 1e9), axis=1, keepdims=True)
            pick = lane_f == first
            sel = jnp.where(pick, 1.0, sel)
            sc = jnp.where(pick, -jnp.inf, sc)
        kn = jnp.concatenate([kn_ref[0], jnp.zeros((128 - ds, ATTN_WIDTH), F32)], axis=0)
        s_own = jnp.where(lane <= row % ds, _dot_nt(scaled_q(), kn.astype(BF16)), NEG)
        m_sel = jnp.max(jnp.where(sel > 0, mx_scr[...], NEG), axis=1, keepdims=True)
        m_fin = jnp.maximum(m_sel, jnp.max(s_own, axis=1, keepdims=True))
        p_own = jnp.exp(s_own - m_fin)
        pown_scr[...] = p_own
        p_sum = p_own
        for blk in range(n_blocks):
            chosen = sel[:, blk:blk + 1] > 0
            for hp in range(ppb):
                p = jnp.where(chosen, jnp.exp(s_scr[blk * ppb + hp] - m_fin), 0.0)
                p_scr[blk * ppb + hp] = p.astype(BF16)
                p_sum = p_sum + p
        l_scr[...] = jnp.broadcast_to(jnp.sum(p_sum, axis=1, keepdims=True), l_scr.shape)
        acc_scr[...] = jnp.zeros(acc_scr.shape, F32)

    @pl.when(slot < n_seq)
    def _():
        q = q_ref[0]
        r_i = lax.broadcasted_iota(jnp.int32, (rows, ATTN_WIDTH), 0)
        l_i = lax.broadcasted_iota(jnp.int32, (rows, ATTN_WIDTH), 1)
        qbd_scr[...] = jnp.where(r_i // ds == l_i // HEAD_DIM, jnp.concatenate([q] * N_HEADS, axis=0), 0.0)
        mx_scr[...] = jnp.full(mx_scr.shape, NEG, F32)
        sb_scr[...] = jnp.zeros(sb_scr.shape, F32)

    def score_k_pages(c, ring):
        qb = scaled_q()
        mx = mx_scr[...]
        sb = sb_scr[...]
        for bb in range(pages_per_step // ppb):
            blk = c * (pages_per_step // ppb) + bb
            rmax = jnp.full((rows, 1), NEG, F32)
            rsum = jnp.zeros((rows, 1), F32)
            for hp in range(ppb):
                page = kbuf[ring, bb * ppb + hp].reshape(ATTN_WIDTH, PAGE_SIZE)
                sc = _dot(qb, page.astype(BF16))
                s_scr[blk * ppb + hp] = sc
                rmax = jnp.maximum(rmax, jnp.max(sc, axis=1, keepdims=True))
                rsum = rsum + jnp.sum(sc, axis=1, keepdims=True)
            mx = jnp.where(lane == blk, rmax, mx)
            sb = jnp.where(lane == blk, rsum, sb)
        mx_scr[...] = mx
        sb_scr[...] = sb

    def apply_v_pages(c, ring):
        acc = acc_scr[...]
        for pg in range(pages_per_step):
            p = p_scr[c * pages_per_step + pg]
            vt = vbuf[ring, pg].reshape(ATTN_WIDTH, PAGE_SIZE).astype(BF16)
            acc = acc + _dot_nt(p, vt)
        acc_scr[...] = acc

    def chunk(c, carry):
        g = slot * n_chunks + c
        ring = g % SAMPLE_RING_SLOTS

        @pl.when(g + lookahead < n_global)
        def _():
            page_copies(g + lookahead, start=True)

        page_copies(g, start=False)

        @pl.when((slot > 0) & (slot < n_seq))
        def _():
            score_k_pages(c, ring)
            apply_v_pages(c, ring)

        @pl.when(slot == 0)
        def _():
            score_k_pages(c, ring)

        @pl.when(slot == n_seq)
        def _():
            apply_v_pages(c, ring)

        return carry

    lax.fori_loop(0, n_chunks, chunk, 0)

    @pl.when(slot > 0)
    def _():
        vn = jnp.concatenate([vn_ref[0], jnp.zeros((128 - ds, ATTN_WIDTH), F32)], axis=0)
        full = acc_scr[...] + _dot(pown_scr[...].astype(BF16), vn.astype(BF16))
        full = full / l_scr[:, 0:1]
        l_i = lax.broadcasted_iota(jnp.int32, (ds, ATTN_WIDTH), 1)
        out = jnp.zeros((ds, ATTN_WIDTH), F32)
        for hh in range(N_HEADS):
            out = out + jnp.where(l_i // HEAD_DIM == hh, full[hh * ds:(hh + 1) * ds, :], 0.0)
        o_ref[0] = out


def _moba_sample(q, k_new, v_new, cache_k, cache_v, page_table):
    db, ds, w = q.shape
    n_pages = page_table.shape[1]
    assert (n_pages * PAGE_SIZE) % MOBA_BLOCK == 0, "past length must be whole MoBA blocks"
    pps = SAMPLE_PAGES_PER_STEP
    n_chunks = n_pages // pps
    assert n_chunks * pps == n_pages and pps % (MOBA_BLOCK // PAGE_SIZE) == 0
    rows = N_HEADS * ds
    ck = jnp.transpose(cache_k, (0, 2, 3, 1))
    cv = jnp.transpose(cache_v, (0, 2, 3, 1))
    pt = page_table.reshape(-1)
    ring_buf = pltpu.VMEM((SAMPLE_RING_SLOTS, pps, N_HEADS, HEAD_DIM, PAGE_SIZE), F32)

    cur = pl.BlockSpec((1, ds, w), lambda slot, pt_ref: (jnp.minimum(slot, db - 1), 0, 0))
    prev = pl.BlockSpec((1, ds, w), lambda slot, pt_ref: (jnp.maximum(slot - 1, 0), 0, 0))
    hbm = pl.BlockSpec(memory_space=pl.ANY)
    grid_spec = pltpu.PrefetchScalarGridSpec(
        num_scalar_prefetch=1,
        grid=(db + 1,),
        in_specs=[cur, prev, prev, hbm, hbm],
        out_specs=prev,
        scratch_shapes=[ring_buf, ring_buf,
                        pltpu.SemaphoreType.DMA((SAMPLE_RING_SLOTS,)),
                        pltpu.SemaphoreType.DMA((SAMPLE_RING_SLOTS,)),
                        pltpu.VMEM((rows, w), F32),
                        pltpu.VMEM((n_pages, rows, PAGE_SIZE), F32),
                        pltpu.VMEM((rows, 128), F32),
                        pltpu.VMEM((rows, 128), F32),
                        pltpu.VMEM((n_pages, rows, PAGE_SIZE), BF16),
                        pltpu.VMEM((rows, 128), F32),
                        pltpu.VMEM((rows, 128), F32),
                        pltpu.VMEM((rows, w), F32)])
    body = functools.partial(_moba_sample_body, n_seq=db, n_chunks=n_chunks, pages_per_step=pps)
    return pl.pallas_call(
        body,
        grid_spec=grid_spec,
        out_shape=jax.ShapeDtypeStruct((db, ds, w), F32),
        compiler_params=pltpu.CompilerParams(dimension_semantics=("arbitrary",),
                                             vmem_limit_bytes=VMEM_LIMIT),
        name="moba_sample",
    )(pt, q, k_new, v_new, ck, cv)


def _mixer_out_body(x_ref, yc_ref, ya_ref, ag_ref, gmc_ref, gma_ref, gate_ref, wb_ref, wo_ref, fg_ref, o_ref):
    gb, tb, d = x_ref.shape
    m = gb * tb
    yc = yc_ref[...].reshape(m, CONV_CH).astype(BF16)
    ya = (ya_ref[...] * _silu(ag_ref[...])).reshape(m, ATTN_WIDTH).astype(BF16)
    merged = (jax.nn.sigmoid(gmc_ref[...].reshape(m, d)) * _dot(yc, wb_ref[0])
              + jax.nn.sigmoid(gma_ref[...].reshape(m, d)) * _dot(ya, wb_ref[1]))
    o = _dot(merged.astype(BF16), wo_ref[...]).reshape(gb, tb, d)
    xo = x_ref[...] + gate_ref[...] * o
    r = xo * lax.rsqrt(jnp.mean(xo * xo, axis=-1, keepdims=True) + EPS)
    o_ref[...] = r * fg_ref[...]


def _mixer_out(x3, y_conv, y_attn, ag, gmc, gma, gate, wb_bf, wo_bf, final_g, gb, tb):
    g, t, d = x3.shape
    xmap = lambda i, j: (i, j, 0)
    bmap = lambda i, j: (i, 0, 0)
    spec = lambda w: pl.BlockSpec((gb, tb, w), xmap)
    return pl.pallas_call(
        _mixer_out_body,
        grid=(g // gb, t // tb),
        in_specs=[spec(d), spec(CONV_CH), spec(ATTN_WIDTH), spec(ATTN_WIDTH), spec(d), spec(d),
                  pl.BlockSpec((gb, 1, d), bmap),
                  pl.BlockSpec((2, CONV_CH, d), lambda i, j: (0, 0, 0)),
                  pl.BlockSpec((d, d), lambda i, j: (0, 0)),
                  pl.BlockSpec((1, d), lambda i, j: (0, 0))],
        out_specs=spec(d),
        out_shape=jax.ShapeDtypeStruct((g, t, d), F32),
        compiler_params=pltpu.CompilerParams(dimension_semantics=("parallel", "parallel"),
                                             vmem_limit_bytes=VMEM_LIMIT),
        name="mixer_out",
    )(x3, y_conv, y_attn, ag, gmc, gma, gate, wb_bf, wo_bf, final_g)


def kernel(x_prompt, x_sample, cache_k, cache_v, state_conv, page_table, c_prompt, c_sample, norm_g, w_ada,
           b_ada, w_in, conv_w, conv_b, conv_ln_g, conv_ln_b, w_branch, w_out, final_g):
    depth = norm_g.shape[0]
    assert depth == 1, "single-layer trunk"
    b, s, d = x_prompt.shape
    db, ds, _ = x_sample.shape
    assert CONV_CH == ATTN_WIDTH and s % MOBA_BLOCK == 0 and s >= CONV_WIDTH - 1
    past_len = page_table.shape[1] * PAGE_SIZE
    lyr = 0

    ada = _ada(jnp.concatenate([c_prompt, c_sample], axis=0), w_ada[lyr], b_ada[lyr])
    shift, scale, gate = (ada[:, j * d:(j + 1) * d][:, None, :] for j in range(3))
    w_in_bf = w_in[lyr].astype(BF16)
    wb_bf = w_branch[lyr].astype(BF16)
    wo_bf = w_out[lyr].astype(BF16)
    g_in = norm_g[lyr].reshape(1, d)
    g_fin = final_g.reshape(1, d)
    cw, cb = conv_w[lyr], conv_b[lyr].reshape(1, CONV_CH)
    lg, lb = conv_ln_g[lyr].reshape(1, CONV_CH), conv_ln_b[lyr].reshape(1, CONV_CH)

    tab_p = _rope_tables(jnp.arange(s))
    glu, cg, q, k, v, ag, gmc, gma = _mixer_in(x_prompt, scale[:b], shift[:b], g_in, w_in_bf, tab_p, 1, ROW_TILE,
                                               paged_kv=True)
    y_conv = _conv_prompt(glu, cg, cw, cb, lg, lb)
    y_attn = _moba_prompt(q, k, v)
    y_prompt = _mixer_out(x_prompt, y_conv, y_attn, ag, gmc, gma, gate[:b], wb_bf, wo_bf, g_fin, 1, ROW_TILE)
    k_prompt = jnp.transpose(k, (0, 1, 4, 2, 3))[None]
    v_prompt = jnp.transpose(v, (0, 1, 4, 2, 3))[None]
    conv_prompt = glu[None, :, s - (CONV_WIDTH - 1):, :]

    gbs = ROW_TILE // ds
    tab_s = tuple(jnp.tile(a, (gbs, 1)) for a in _rope_tables(past_len + jnp.arange(ds)))
    glu, cg, q, k, v, ag, gmc, gma = _mixer_in(x_sample, scale[b:], shift[b:], g_in, w_in_bf, tab_s, gbs, ds,
                                               paged_kv=False)
    hist = jnp.concatenate([state_conv[lyr], glu], axis=1)
    y_conv = _conv_sample(hist, cg, cw, cb, lg, lb)
    y_attn = _moba_sample(q, k, v, cache_k[lyr], cache_v[lyr], page_table)
    y_sample = _mixer_out(x_sample, y_conv, y_attn, ag, gmc, gma, gate[b:], wb_bf, wo_bf, g_fin, gbs, ds)
    k_sample = k.reshape(1, db, ds, N_HEADS, HEAD_DIM)
    v_sample = v.reshape(1, db, ds, N_HEADS, HEAD_DIM)
    conv_sample = hist[None, :, ds:, :]

    return (y_prompt, y_sample, k_prompt, v_prompt, conv_prompt, k_sample, v_sample, conv_sample)
```

```python
import functools

import jax
import jax.numpy as jnp
from jax import lax
from jax.experimental import pallas as pl
from jax.experimental.pallas import tpu as pltpu

F32 = jnp.float32
BF16 = jnp.bfloat16

D_MODEL = 1024
HEAD_DIM = 64
N_HEADS = 8
ATTN_WIDTH = N_HEADS * HEAD_DIM
CONV_CH = 512
ROT_DIM = HEAD_DIM // 4
ROPE_THETA = 500000.0
MOBA_BLOCK = 256
MOBA_TOPK = 3
CONV_WIDTH = 31
PAGE_SIZE = 128
EPS = 1e-6
IN_SIZES = (CONV_CH, CONV_CH, CONV_CH, ATTN_WIDTH, ATTN_WIDTH, ATTN_WIDTH, ATTN_WIDTH, D_MODEL, D_MODEL)
IN_COLS = sum(IN_SIZES)
NEG = -1e30

ROW_TILE = 256
CONV_TILE = 256
CONV_SUB = 32
HIST_PAD = 32
PROMPT_HEADS_PER_STEP = 4
SAMPLE_PAGES_PER_STEP = 8
SAMPLE_RING_SLOTS = 4
VMEM_LIMIT = 56 * 1024 * 1024


def _silu(x):
    return x * jax.nn.sigmoid(x)


def _dot(a, b):
    return jnp.dot(a, b, preferred_element_type=F32)


def _dot_nt(a, b, precision=None):
    return lax.dot_general(a, b, (((1,), (1,)), ((), ())), precision=precision,
                           preferred_element_type=F32)


def _ada_body(c_ref, w_ref, b_ref, o_ref):
    a = _silu(c_ref[...]).astype(BF16)
    o_ref[...] = _dot(a, w_ref[...].astype(BF16)) + b_ref[...]


def _ada(c_all, w_ada, b_ada):
    n = c_all.shape[0]
    return pl.pallas_call(
        _ada_body,
        grid=(3,),
        in_specs=[pl.BlockSpec((n, D_MODEL), lambda j: (0, 0)),
                  pl.BlockSpec((D_MODEL, D_MODEL), lambda j: (0, j)),
                  pl.BlockSpec((1, D_MODEL), lambda j: (0, j))],
        out_specs=pl.BlockSpec((n, D_MODEL), lambda j: (0, j)),
        out_shape=jax.ShapeDtypeStruct((n, 3 * D_MODEL), F32),
        compiler_params=pltpu.CompilerParams(dimension_semantics=("arbitrary",),
                                             vmem_limit_bytes=VMEM_LIMIT),
        name="ada",
    )(c_all, w_ada, b_ada.reshape(1, 3 * D_MODEL))


def _rope_rows(x, cos, sin_lo, sin_hi):
    n = x.shape[-1]
    half = ROT_DIM // 2
    return x * cos + pltpu.roll(x, n - half, 1) * sin_lo + pltpu.roll(x, half, 1) * sin_hi


def _mixer_in_body(x_ref, scale_ref, shift_ref, g_ref, w_ref, cos_ref, slo_ref, shi_ref,
                   glu_ref, cg_ref, q_ref, k_ref, v_ref, ag_ref, gmc_ref, gma_ref, *, paged_kv):
    gb, tb, d = x_ref.shape
    m = gb * tb
    x = x_ref[...]
    r = x * lax.rsqrt(jnp.mean(x * x, axis=-1, keepdims=True) + EPS)
    h = (r * g_ref[...]) * (1.0 + scale_ref[...]) + shift_ref[...]
    hb = h.reshape(m, d).astype(BF16)

    def seg(idx):
        lo = sum(IN_SIZES[:idx])
        return _dot(hb, w_ref[:, lo:lo + IN_SIZES[idx]])

    def put(ref, val):
        ref[...] = val.reshape(ref.shape)

    def put_kv(ref, val):
        if not paged_kv:
            return put(ref, val)
        val_t = val.T
        for pg in range(m // PAGE_SIZE):
            ref[0, pg] = val_t[:, pg * PAGE_SIZE:(pg + 1) * PAGE_SIZE].reshape(N_HEADS, HEAD_DIM, PAGE_SIZE)

    put(glu_ref, seg(0) * jax.nn.sigmoid(seg(1)))
    put(cg_ref, seg(2))
    cos, slo, shi = cos_ref[...], slo_ref[...], shi_ref[...]
    put(q_ref, _rope_rows(seg(3), cos, slo, shi))
    put_kv(k_ref, _rope_rows(seg(4), cos, slo, shi))
    put_kv(v_ref, seg(5))
    put(ag_ref, seg(6))
    put(gmc_ref, seg(7))
    put(gma_ref, seg(8))


def _mixer_in(x3, scale, shift, norm_g, w_in_bf, tables, gb, tb, paged_kv):
    g, t, d = x3.shape
    m = gb * tb
    grid = (g // gb, t // tb)
    xmap = lambda i, j: (i, j, 0)
    bmap = lambda i, j: (i, 0, 0)
    cmap = lambda i, j: (0, 0)
    tmap = lambda i, j: (j, 0)
    widths = (CONV_CH, CONV_CH, ATTN_WIDTH, ATTN_WIDTH, ATTN_WIDTH, ATTN_WIDTH, D_MODEL, D_MODEL)
    out_specs = [pl.BlockSpec((gb, tb, w), xmap) for w in widths]
    out_shape = [jax.ShapeDtypeStruct((g, t, w), F32) for w in widths]
    if paged_kv:
        assert gb == 1 and tb % PAGE_SIZE == 0
        for idx in (3, 4):
            out_specs[idx] = pl.BlockSpec((1, tb // PAGE_SIZE, N_HEADS, HEAD_DIM, PAGE_SIZE),
                                          lambda i, j: (i, j, 0, 0, 0))
            out_shape[idx] = jax.ShapeDtypeStruct((g, t // PAGE_SIZE, N_HEADS, HEAD_DIM, PAGE_SIZE), F32)
    return pl.pallas_call(
        functools.partial(_mixer_in_body, paged_kv=paged_kv),
        grid=grid,
        in_specs=[pl.BlockSpec((gb, tb, d), xmap),
                  pl.BlockSpec((gb, 1, d), bmap),
                  pl.BlockSpec((gb, 1, d), bmap),
                  pl.BlockSpec((1, d), cmap),
                  pl.BlockSpec((d, IN_COLS), cmap, pipeline_mode=pl.Buffered(1)),
                  pl.BlockSpec((m, ATTN_WIDTH), tmap),
                  pl.BlockSpec((m, ATTN_WIDTH), tmap),
                  pl.BlockSpec((m, ATTN_WIDTH), tmap)],
        out_specs=out_specs,
        out_shape=out_shape,
        compiler_params=pltpu.CompilerParams(dimension_semantics=("parallel", "parallel"),
                                             vmem_limit_bytes=VMEM_LIMIT),
        name="mixer_in",
    )(x3, scale, shift, norm_g, w_in_bf, *tables)


def _rope_tables(pos):
    inv = ROPE_THETA ** (-(jnp.arange(0, ROT_DIM, 2, dtype=F32) / ROT_DIM))
    ang = pos.astype(F32)[:, None] * inv[None, :]
    cos, sin = jnp.cos(ang), jnp.sin(ang)
    half = ROT_DIM // 2
    n = pos.shape[0]
    pad = jnp.zeros((n, HEAD_DIM - ROT_DIM), F32)
    zero = jnp.zeros((n, half), F32)
    cos_h = jnp.concatenate([cos, cos, pad + 1.0], axis=1)
    slo_h = jnp.concatenate([-sin, zero, pad], axis=1)
    shi_h = jnp.concatenate([zero, sin, pad], axis=1)
    return tuple(jnp.tile(a, (1, N_HEADS)) for a in (cos_h, slo_h, shi_h))


def _conv_epilogue(y, cg, b, lg, lb):
    y = y + b
    mu = jnp.mean(y, axis=-1, keepdims=True)
    yc = y - mu
    var = jnp.mean(yc * yc, axis=-1, keepdims=True)
    z = _silu(yc * lax.rsqrt(var + EPS) * lg + lb)
    return z * _silu(cg)


def _conv_tap_offsets():
    off = HIST_PAD - (CONV_WIDTH - 1)
    return [divmod(off + tap, 8) for tap in range(CONV_WIDTH)]


def _conv_shift_span(tc):
    return tc + 8 * max(a for a, r in _conv_tap_offsets() if r)


def _conv_prompt_body(cur_ref, prev_ref, cg_ref, w_ref, b_ref, lg_ref, lb_ref, o_ref, hist_scr, shift_scr):
    j = pl.program_id(1)
    tc = cur_ref.shape[1]
    hist_scr[0:HIST_PAD, :] = jnp.where(j > 0, prev_ref[0], 0.0)
    hist_scr[HIST_PAD:HIST_PAD + tc, :] = cur_ref[0]
    span = _conv_shift_span(tc)
    for r in range(1, 8):
        shift_scr[r - 1] = hist_scr[r:r + span, :]
    for c in range(tc // CONV_SUB):
        r0 = c * CONV_SUB
        acc = jnp.zeros((CONV_SUB, CONV_CH), F32)
        for tap, (a, r) in enumerate(_conv_tap_offsets()):
            lo = r0 + 8 * a
            win = hist_scr[lo:lo + CONV_SUB, :] if r == 0 else shift_scr[r - 1, lo:lo + CONV_SUB, :]
            acc = acc + w_ref[tap:tap + 1, :] * win
        o_ref[0, r0:r0 + CONV_SUB, :] = _conv_epilogue(
            acc, cg_ref[0, r0:r0 + CONV_SUB, :], b_ref[...], lg_ref[...], lb_ref[...])


def _conv_prompt(glu, cg, conv_w, conv_b, ln_g, ln_b):
    b, s, c = glu.shape
    tc = CONV_TILE
    per = tc // HIST_PAD
    cmap = lambda i, j: (0, 0)
    return pl.pallas_call(
        _conv_prompt_body,
        grid=(b, s // tc),
        in_specs=[pl.BlockSpec((1, tc, c), lambda i, j: (i, j, 0)),
                  pl.BlockSpec((1, HIST_PAD, c), lambda i, j: (i, jnp.maximum(j * per - 1, 0), 0)),
                  pl.BlockSpec((1, tc, c), lambda i, j: (i, j, 0)),
                  pl.BlockSpec((CONV_WIDTH, c), cmap),
                  pl.BlockSpec((1, c), cmap),
                  pl.BlockSpec((1, c), cmap),
                  pl.BlockSpec((1, c), cmap)],
        out_specs=pl.BlockSpec((1, tc, c), lambda i, j: (i, j, 0)),
        out_shape=jax.ShapeDtypeStruct((b, s, c), F32),
        scratch_shapes=[pltpu.VMEM((HIST_PAD + tc, c), F32),
                        pltpu.VMEM((7, _conv_shift_span(tc), c), F32)],
        compiler_params=pltpu.CompilerParams(dimension_semantics=("parallel", "parallel"),
                                             vmem_limit_bytes=VMEM_LIMIT),
        name="conv_prompt",
    )(glu, glu, cg, conv_w, conv_b, ln_g, ln_b)


def _conv_sample_body(hist_ref, cg_ref, w_ref, b_ref, lg_ref, lb_ref, o_ref):
    nb, t, c = cg_ref.shape
    acc = jnp.zeros((nb, t, c), F32)
    for tap in range(CONV_WIDTH):
        acc = acc + w_ref[tap:tap + 1, :] * hist_ref[:, tap:tap + t, :]
    o_ref[...] = _conv_epilogue(acc, cg_ref[...], b_ref[...], lg_ref[...], lb_ref[...])


def _conv_sample(hist, cg, conv_w, conv_b, ln_g, ln_b):
    n, t, c = cg.shape
    nb = 8
    cmap = lambda i: (0, 0)
    return pl.pallas_call(
        _conv_sample_body,
        grid=(n // nb,),
        in_specs=[pl.BlockSpec((nb, hist.shape[1], c), lambda i: (i, 0, 0)),
                  pl.BlockSpec((nb, t, c), lambda i: (i, 0, 0)),
                  pl.BlockSpec((CONV_WIDTH, c), cmap),
                  pl.BlockSpec((1, c), cmap),
                  pl.BlockSpec((1, c), cmap),
                  pl.BlockSpec((1, c), cmap)],
        out_specs=pl.BlockSpec((nb, t, c), lambda i: (i, 0, 0)),
        out_shape=jax.ShapeDtypeStruct((n, t, c), F32),
        compiler_params=pltpu.CompilerParams(dimension_semantics=("parallel",),
                                             vmem_limit_bytes=VMEM_LIMIT),
        name="conv_sample",
    )(hist, cg, conv_w, conv_b, ln_g, ln_b)


def _prompt_attention_stages(i, q_ref, k_ref, v_ref, o_ref, kb_scr, vt_scr, kmean_scr, bias_scr, s_scr, acc_scr,
                             pmx_scr):
    tq = q_ref.shape[1]
    n_pages, heads = k_ref.shape[1], k_ref.shape[2]
    ppb = MOBA_BLOCK // PAGE_SIZE
    nblk = n_pages // ppb

    def fold(x, op):
        return op(x.reshape(MOBA_BLOCK // 8, 8, tq), axis=0)

    def prepare():
        for blk in range(nblk):
            ksum = jnp.zeros((1, heads * HEAD_DIM), F32)
            for hp in range(ppb):
                pg = blk * ppb + hp
                k_rows = k_ref[0, pg].reshape(heads * HEAD_DIM, PAGE_SIZE).T
                ksum = ksum + jnp.sum(k_rows, axis=0, keepdims=True)
                for hh in range(heads):
                    kb_scr[hh, pg * PAGE_SIZE:(pg + 1) * PAGE_SIZE, :] = (
                        k_rows[:, hh * HEAD_DIM:(hh + 1) * HEAD_DIM].astype(BF16))
            kmean_scr[blk:blk + 1, :] = ksum * (1.0 / MOBA_BLOCK)
            for hh in range(heads):
                vt_scr[hh, blk] = jnp.concatenate(
                    [v_ref[0, blk * ppb + hp, hh] for hp in range(ppb)], axis=1).astype(BF16)

    def stage1():
        pl.when(i == 0)(prepare)
        kmean = kmean_scr[...]
        blk_iota = lax.broadcasted_iota(jnp.int32, (nblk, tq), 0)
        past = blk_iota < i
        causal = (lax.broadcasted_iota(jnp.int32, (MOBA_BLOCK, tq), 0)
                  <= lax.broadcasted_iota(jnp.int32, (MOBA_BLOCK, tq), 1))
        qbs = []
        for hh in range(heads):
            q_h = q_ref[0, :, hh * HEAD_DIM:(hh + 1) * HEAD_DIM]
            s_blk = _dot_nt(kmean[:, hh * HEAD_DIM:(hh + 1) * HEAD_DIM], q_h,
                            precision=lax.Precision.HIGHEST)
            s_blk = jnp.where(past, s_blk, -jnp.inf)
            rank = jnp.zeros((nblk, tq), jnp.int32)
            for mm in range(nblk):
                sm = s_blk[mm:mm + 1, :]
                rank = rank + jnp.where(sm > s_blk, 1, jnp.where((sm == s_blk) & (blk_iota > mm), 1, 0))
            bias_scr[hh] = jnp.where(past & (rank < MOBA_TOPK), 0.0, NEG)
            qbs.append((q_h * (HEAD_DIM ** -0.5)).astype(BF16))

        def scores(hh, j):
            kj = kb_scr[hh, pl.ds(pl.multiple_of(j * MOBA_BLOCK, MOBA_BLOCK), MOBA_BLOCK), :]
            return _dot_nt(kj, qbs[hh])

        mx_own = []
        for hh in range(heads):
            s_t = jnp.where(causal, scores(hh, i), NEG)
            s_scr[hh, i] = s_t
            mx_own.append(fold(s_t, jnp.max))

        def score_block(j, mx):
            out = []
            for hh in range(heads):
                s_t = scores(hh, j) + bias_scr[hh, pl.ds(j, 1), :]
                s_scr[hh, j] = s_t
                out.append(jnp.maximum(mx[hh], fold(s_t, jnp.max)))
            return tuple(out)

        mx = lax.fori_loop(0, i, score_block, tuple(mx_own))
        for hh in range(heads):
            pmx_scr[hh] = mx[hh]

    def stage2():
        m_fin = [jnp.max(pmx_scr[hh], axis=0, keepdims=True) for hh in range(heads)]
        for hh in range(heads):
            acc_scr[hh] = jnp.zeros((HEAD_DIM, tq), F32)

        def apply_block(j, ls):
            out = []
            for hh in range(heads):
                p = jnp.exp(s_scr[hh, j] - m_fin[hh])
                out.append(ls[hh] + fold(p, jnp.sum))
                acc_scr[hh] = acc_scr[hh] + _dot(vt_scr[hh, j], p.astype(BF16))
            return tuple(out)

        ls = lax.fori_loop(0, i + 1, apply_block, tuple(jnp.zeros((8, tq), F32) for _ in range(heads)))
        o_ref[0] = jnp.concatenate(
            [acc_scr[hh] / jnp.sum(ls[hh], axis=0, keepdims=True) for hh in range(heads)], axis=0).T

    return stage1, stage2


def _prompt_scratch(s, tq, heads):
    nblk = s // MOBA_BLOCK
    return [pltpu.VMEM((heads, s, HEAD_DIM), BF16),
            pltpu.VMEM((heads, nblk, HEAD_DIM, MOBA_BLOCK), BF16),
            pltpu.VMEM((nblk, heads * HEAD_DIM), F32),
            pltpu.VMEM((heads, nblk, tq), F32),
            pltpu.VMEM((heads, nblk, MOBA_BLOCK, tq), F32),
            pltpu.VMEM((heads, HEAD_DIM, tq), F32),
            pltpu.VMEM((heads, 8, tq), F32)]


def _sample_stream(slot, pt_ref, q_ref, kn_ref, vn_ref, ck_hbm, cv_hbm, o_ref,
                   kbuf, vbuf, ksem, vsem, qbd_scr, s_scr, mx_scr, sb_scr, p_scr, pown_scr, l_scr, acc_scr,
                   *, n_seq, n_chunks, pages_per_step):
    ds = q_ref.shape[1]
    rows = N_HEADS * ds
    ppb = MOBA_BLOCK // PAGE_SIZE
    n_pages = n_chunks * pages_per_step
    n_blocks = n_pages // ppb
    n_global = (n_seq + 1) * n_chunks
    lookahead = SAMPLE_RING_SLOTS - 1
    lane = lax.broadcasted_iota(jnp.int32, (rows, 128), 1)
    row = lax.broadcasted_iota(jnp.int32, (rows, 128), 0)

    def page_copies(g, start):
        g_slot = g // n_chunks
        g_c = g % n_chunks
        ring = g % SAMPLE_RING_SLOTS

        def each(cache, buf, sem, seq):
            for p in range(pages_per_step):
                if start:
                    page = pt_ref[seq * n_pages + g_c * pages_per_step + p]
                    pltpu.make_async_copy(cache.at[page], buf.at[ring, p], sem.at[ring]).start()
                else:
                    pltpu.make_async_copy(cache.at[0], buf.at[ring, p], sem.at[ring]).wait()

        @pl.when(g_slot < n_seq)
        def _():
            each(ck_hbm, kbuf, ksem, g_slot)

        @pl.when(g_slot > 0)
        def _():
            each(cv_hbm, vbuf, vsem, g_slot - 1)

    def prime():
        for g in range(lookahead):
            page_copies(jnp.int32(g), start=True)

    def scaled_q():
        return (qbd_scr[...] * (HEAD_DIM ** -0.5)).astype(BF16)

    def finalize():
        sc = jnp.where(lane < n_blocks, sb_scr[...], -jnp.inf)
        lane_f = lane.astype(F32)
        sel = jnp.zeros((rows, 128), F32)
        for _ in range(min(MOBA_TOPK, n_blocks)):
            top = jnp.max(sc, axis=1, keepdims=True)
            first = jnp.min(jnp.where(sc == top, lane_f, 1e9), axis=1, keepdims=True)
            pick = lane_f == first
            sel = jnp.where(pick, 1.0, sel)
            sc = jnp.where(pick, -jnp.inf, sc)
        kn = jnp.concatenate([kn_ref[0], jnp.zeros((128 - ds, ATTN_WIDTH), F32)], axis=0)
        s_own = jnp.where(lane <= row % ds, _dot_nt(scaled_q(), kn.astype(BF16)), NEG)
        m_sel = jnp.max(jnp.where(sel > 0, mx_scr[...], NEG), axis=1, keepdims=True)
        m_fin = jnp.maximum(m_sel, jnp.max(s_own, axis=1, keepdims=True))
        p_own = jnp.exp(s_own - m_fin)
        pown_scr[...] = p_own
        p_sum = p_own
        for blk in range(n_blocks):
            chosen = sel[:, blk:blk + 1] > 0
            for hp in range(ppb):
                p = jnp.where(chosen, jnp.exp(s_scr[blk * ppb + hp] - m_fin), 0.0)
                p_scr[blk * ppb + hp] = p.astype(BF16)
                p_sum = p_sum + p
        l_scr[...] = jnp.broadcast_to(jnp.sum(p_sum, axis=1, keepdims=True), l_scr.shape)
        acc_scr[...] = jnp.zeros(acc_scr.shape, F32)

    def load_queries():
        q = q_ref[0]
        r_i = lax.broadcasted_iota(jnp.int32, (rows, ATTN_WIDTH), 0)
        l_i = lax.broadcasted_iota(jnp.int32, (rows, ATTN_WIDTH), 1)
        qbd_scr[...] = jnp.where(r_i // ds == l_i // HEAD_DIM, jnp.concatenate([q] * N_HEADS, axis=0), 0.0)
        mx_scr[...] = jnp.full(mx_scr.shape, NEG, F32)
        sb_scr[...] = jnp.zeros(sb_scr.shape, F32)

    def score_k_pages(c, ring):
        qb = scaled_q()
        mx = mx_scr[...]
        sb = sb_scr[...]
        for bb in range(pages_per_step // ppb):
            blk = c * (pages_per_step // ppb) + bb
            rmax = jnp.full((rows, 1), NEG, F32)
            rsum = jnp.zeros((rows, 1), F32)
            for hp in range(ppb):
                page = kbuf[ring, bb * ppb + hp].reshape(ATTN_WIDTH, PAGE_SIZE)
                sc = _dot(qb, page.astype(BF16))
                s_scr[blk * ppb + hp] = sc
                rmax = jnp.maximum(rmax, jnp.max(sc, axis=1, keepdims=True))
                rsum = rsum + jnp.sum(sc, axis=1, keepdims=True)
            mx = jnp.where(lane == blk, rmax, mx)
            sb = jnp.where(lane == blk, rsum, sb)
        mx_scr[...] = mx
        sb_scr[...] = sb

    def apply_v_pages(c, ring):
        acc = acc_scr[...]
        for pg in range(pages_per_step):
            p = p_scr[c * pages_per_step + pg]
            vt = vbuf[ring, pg].reshape(ATTN_WIDTH, PAGE_SIZE).astype(BF16)
            acc = acc + _dot_nt(p, vt)
        acc_scr[...] = acc

    def chunk(c, carry):
        g = slot * n_chunks + c
        ring = g % SAMPLE_RING_SLOTS

        @pl.when(g + lookahead < n_global)
        def _():
            page_copies(g + lookahead, start=True)

        page_copies(g, start=False)

        @pl.when((slot > 0) & (slot < n_seq))
        def _():
            score_k_pages(c, ring)
            apply_v_pages(c, ring)

        @pl.when(slot == 0)
        def _():
            score_k_pages(c, ring)

        @pl.when(slot == n_seq)
        def _():
            apply_v_pages(c, ring)

        return carry

    def write_output():
        vn = jnp.concatenate([vn_ref[0], jnp.zeros((128 - ds, ATTN_WIDTH), F32)], axis=0)
        full = acc_scr[...] + _dot(pown_scr[...].astype(BF16), vn.astype(BF16))
        full = full / l_scr[:, 0:1]
        l_i = lax.broadcasted_iota(jnp.int32, (ds, ATTN_WIDTH), 1)
        out = jnp.zeros((ds, ATTN_WIDTH), F32)
        for hh in range(N_HEADS):
            out = out + jnp.where(l_i // HEAD_DIM == hh, full[hh * ds:(hh + 1) * ds, :], 0.0)
        o_ref[0] = out

    def begin():
        pl.when(slot == 0)(prime)
        pl.when(slot > 0)(finalize)
        pl.when(slot < n_seq)(load_queries)

    def run_chunks(c0, c1):
        lax.fori_loop(c0, c1, chunk, 0)

    def end():
        pl.when(slot > 0)(write_output)

    return begin, run_chunks, end


def _sample_scratch(ds, n_pages, pps):
    rows = N_HEADS * ds
    ring_buf = pltpu.VMEM((SAMPLE_RING_SLOTS, pps, N_HEADS, HEAD_DIM, PAGE_SIZE), F32)
    return [ring_buf, ring_buf,
            pltpu.SemaphoreType.DMA((SAMPLE_RING_SLOTS,)),
            pltpu.SemaphoreType.DMA((SAMPLE_RING_SLOTS,)),
            pltpu.VMEM((rows, ATTN_WIDTH), F32),
            pltpu.VMEM((n_pages, rows, PAGE_SIZE), F32),
            pltpu.VMEM((rows, 128), F32),
            pltpu.VMEM((rows, 128), F32),
            pltpu.VMEM((n_pages, rows, PAGE_SIZE), BF16),
            pltpu.VMEM((rows, 128), F32),
            pltpu.VMEM((rows, 128), F32),
            pltpu.VMEM((rows, ATTN_WIDTH), F32)]


def _moba_body(pt_ref, qs_ref, kn_ref, vn_ref, ck_hbm, cv_hbm, qp_ref, kp_ref, vp_ref, os_ref, op_ref, *scratch,
               n_seq, n_chunks, pages_per_step, n_prompt_steps, n_qtiles, n_sample_scratch):
    t = pl.program_id(0)
    begin, run_chunks, end = _sample_stream(
        t, pt_ref, qs_ref, kn_ref, vn_ref, ck_hbm, cv_hbm, os_ref, *scratch[:n_sample_scratch],
        n_seq=n_seq, n_chunks=n_chunks, pages_per_step=pages_per_step)
    stage1, stage2 = _prompt_attention_stages(t % n_qtiles, qp_ref, kp_ref, vp_ref, op_ref,
                                              *scratch[n_sample_scratch:])
    n_steps = max(n_seq + 1, n_prompt_steps)
    sample_part = (lambda f: f()) if n_steps == n_seq + 1 else (lambda f: pl.when(t <= n_seq)(f))
    prompt_part = (lambda f: f()) if n_steps == n_prompt_steps else (lambda f: pl.when(t < n_prompt_steps)(f))
    c1, c2 = n_chunks // 4, (5 * n_chunks) // 8

    def sample_head():
        begin()
        run_chunks(0, c1)

    def sample_middle():
        run_chunks(c1, c2)

    def sample_tail():
        run_chunks(c2, n_chunks)
        end()

    sample_part(sample_head)
    prompt_part(stage1)
    sample_part(sample_middle)
    prompt_part(stage2)
    sample_part(sample_tail)


def _moba(q_p, k_paged, v_paged, q_s, k_new, v_new, cache_k, cache_v, page_table):
    b, s, w = q_p.shape
    db, ds, _ = q_s.shape
    n_pages = page_table.shape[1]
    assert (n_pages * PAGE_SIZE) % MOBA_BLOCK == 0, "past length must be whole MoBA blocks"
    pps = SAMPLE_PAGES_PER_STEP
    n_chunks = n_pages // pps
    assert n_chunks * pps == n_pages and pps % (MOBA_BLOCK // PAGE_SIZE) == 0
    ck = jnp.transpose(cache_k, (0, 2, 3, 1))
    cv = jnp.transpose(cache_v, (0, 2, 3, 1))
    pt = page_table.reshape(-1)

    tq = MOBA_BLOCK
    heads = PROMPT_HEADS_PER_STEP
    lanes = heads * HEAD_DIM
    n_qt, n_hg = s // tq, w // lanes
    n_prompt = b * n_hg * n_qt
    n_steps = max(db + 1, n_prompt)

    def prompt_index(t):
        tp = jnp.minimum(t, n_prompt - 1)
        return tp // (n_hg * n_qt), (tp // n_qt) % n_hg, tp % n_qt

    def q_map(t, pt_ref):
        bi, hg, i = prompt_index(t)
        return (bi, i, hg)

    def kv_map(t, pt_ref):
        bi, hg, _ = prompt_index(t)
        return (bi, 0, hg, 0, 0)

    cur = pl.BlockSpec((1, ds, w), lambda t, pt_ref: (jnp.minimum(t, db - 1), 0, 0))
    prev = pl.BlockSpec((1, ds, w), lambda t, pt_ref: (jnp.clip(t - 1, 0, db - 1), 0, 0))
    hbm = pl.BlockSpec(memory_space=pl.ANY)
    q_spec = pl.BlockSpec((1, tq, lanes), q_map)
    kv_spec = pl.BlockSpec((1, s // PAGE_SIZE, heads, HEAD_DIM, PAGE_SIZE), kv_map)
    sample_scratch = _sample_scratch(ds, n_pages, pps)
    grid_spec = pltpu.PrefetchScalarGridSpec(
        num_scalar_prefetch=1,
        grid=(n_steps,),
        in_specs=[cur, prev, prev, hbm, hbm, q_spec, kv_spec, kv_spec],
        out_specs=[prev, q_spec],
        scratch_shapes=sample_scratch + _prompt_scratch(s, tq, heads))
    body = functools.partial(_moba_body, n_seq=db, n_chunks=n_chunks, pages_per_step=pps,
                             n_prompt_steps=n_prompt, n_qtiles=n_qt, n_sample_scratch=len(sample_scratch))
    y_s, y_p = pl.pallas_call(
        body,
        grid_spec=grid_spec,
        out_shape=[jax.ShapeDtypeStruct((db, ds, w), F32), jax.ShapeDtypeStruct((b, s, w), F32)],
        compiler_params=pltpu.CompilerParams(dimension_semantics=("arbitrary",),
                                             vmem_limit_bytes=VMEM_LIMIT),
        name="moba",
    )(pt, q_s, k_new, v_new, ck, cv, q_p, k_paged, v_paged)
    return y_p, y_s


def _mixer_out_body(x_ref, yc_ref, ya_ref, ag_ref, gmc_ref, gma_ref, gate_ref, wb_ref, wo_ref, fg_ref, o_ref):
    gb, tb, d = x_ref.shape
    m = gb * tb
    yc = yc_ref[...].reshape(m, CONV_CH).astype(BF16)
    ya = (ya_ref[...] * _silu(ag_ref[...])).reshape(m, ATTN_WIDTH).astype(BF16)
    merged = (jax.nn.sigmoid(gmc_ref[...].reshape(m, d)) * _dot(yc, wb_ref[0])
              + jax.nn.sigmoid(gma_ref[...].reshape(m, d)) * _dot(ya, wb_ref[1]))
    o = _dot(merged.astype(BF16), wo_ref[...]).reshape(gb, tb, d)
    xo = x_ref[...] + gate_ref[...] * o
    r = xo * lax.rsqrt(jnp.mean(xo * xo, axis=-1, keepdims=True) + EPS)
    o_ref[...] = r * fg_ref[...]


def _mixer_out(x3, y_conv, y_attn, ag, gmc, gma, gate, wb_bf, wo_bf, final_g, gb, tb):
    g, t, d = x3.shape
    xmap = lambda i, j: (i, j, 0)
    bmap = lambda i, j: (i, 0, 0)
    spec = lambda w: pl.BlockSpec((gb, tb, w), xmap)
    return pl.pallas_call(
        _mixer_out_body,
        grid=(g // gb, t // tb),
        in_specs=[spec(d), spec(CONV_CH), spec(ATTN_WIDTH), spec(ATTN_WIDTH), spec(d), spec(d),
                  pl.BlockSpec((gb, 1, d), bmap),
                  pl.BlockSpec((2, CONV_CH, d), lambda i, j: (0, 0, 0)),
                  pl.BlockSpec((d, d), lambda i, j: (0, 0)),
                  pl.BlockSpec((1, d), lambda i, j: (0, 0))],
        out_specs=spec(d),
        out_shape=jax.ShapeDtypeStruct((g, t, d), F32),
        compiler_params=pltpu.CompilerParams(dimension_semantics=("parallel", "parallel"),
                                             vmem_limit_bytes=VMEM_LIMIT),
        name="mixer_out",
    )(x3, y_conv, y_attn, ag, gmc, gma, gate, wb_bf, wo_bf, final_g)


def kernel(x_prompt, x_sample, cache_k, cache_v, state_conv, page_table, c_prompt, c_sample, norm_g, w_ada,
           b_ada, w_in, conv_w, conv_b, conv_ln_g, conv_ln_b, w_branch, w_out, final_g):
    depth = norm_g.shape[0]
    assert depth == 1, "single-layer trunk"
    b, s, d = x_prompt.shape
    db, ds, _ = x_sample.shape
    assert CONV_CH == ATTN_WIDTH and s % MOBA_BLOCK == 0 and s >= CONV_WIDTH - 1
    past_len = page_table.shape[1] * PAGE_SIZE
    lyr = 0

    ada = _ada(jnp.concatenate([c_prompt, c_sample], axis=0), w_ada[lyr], b_ada[lyr])
    shift, scale, gate = (ada[:, j * d:(j + 1) * d][:, None, :] for j in range(3))
    w_in_bf = w_in[lyr].astype(BF16)
    wb_bf = w_branch[lyr].astype(BF16)
    wo_bf = w_out[lyr].astype(BF16)
    g_in = norm_g[lyr].reshape(1, d)
    g_fin = final_g.reshape(1, d)
    cw, cb = conv_w[lyr], conv_b[lyr].reshape(1, CONV_CH)
    lg, lb = conv_ln_g[lyr].reshape(1, CONV_CH), conv_ln_b[lyr].reshape(1, CONV_CH)

    tab_p = _rope_tables(jnp.arange(s))
    glu_p, cg_p, q_p, k_p, v_p, ag_p, gmc_p, gma_p = _mixer_in(
        x_prompt, scale[:b], shift[:b], g_in, w_in_bf, tab_p, 1, ROW_TILE, paged_kv=True)
    gbs = ROW_TILE // ds
    tab_s = tuple(jnp.tile(a, (gbs, 1)) for a in _rope_tables(past_len + jnp.arange(ds)))
    glu_s, cg_s, q_s, k_s, v_s, ag_s, gmc_s, gma_s = _mixer_in(
        x_sample, scale[b:], shift[b:], g_in, w_in_bf, tab_s, gbs, ds, paged_kv=False)
    yc_p = _conv_prompt(glu_p, cg_p, cw, cb, lg, lb)
    hist = jnp.concatenate([state_conv[lyr], glu_s], axis=1)
    yc_s = _conv_sample(hist, cg_s, cw, cb, lg, lb)

    ya_p, ya_s = _moba(q_p, k_p, v_p, q_s, k_s, v_s, cache_k[lyr], cache_v[lyr], page_table)

    y_prompt = _mixer_out(x_prompt, yc_p, ya_p, ag_p, gmc_p, gma_p, gate[:b], wb_bf, wo_bf, g_fin, 1, ROW_TILE)
    y_sample = _mixer_out(x_sample, yc_s, ya_s, ag_s, gmc_s, gma_s, gate[b:], wb_bf, wo_bf, g_fin, gbs, ds)
    k_prompt = jnp.transpose(k_p, (0, 1, 4, 2, 3))[None]
    v_prompt = jnp.transpose(v_p, (0, 1, 4, 2, 3))[None]
    conv_prompt = glu_p[None, :, s - (CONV_WIDTH - 1):, :]
    k_sample = k_s.reshape(1, db, ds, N_HEADS, HEAD_DIM)
    v_sample = v_s.reshape(1, db, ds, N_HEADS, HEAD_DIM)
    conv_sample = hist[None, :, ds:, :]

    return (y_prompt, y_sample, k_prompt, v_prompt, conv_prompt, k_sample, v_sample, conv_sample)
```

```python
import functools

import jax
import jax.numpy as jnp
from jax import lax
from jax.experimental import pallas as pl
from jax.experimental.pallas import tpu as pltpu

F32 = jnp.float32
BF16 = jnp.bfloat16

D_MODEL = 1024
HEAD_DIM = 64
N_HEADS = 8
ATTN_WIDTH = N_HEADS * HEAD_DIM
CONV_CH = 512
ROT_DIM = HEAD_DIM // 4
ROPE_THETA = 500000.0
MOBA_BLOCK = 256
MOBA_TOPK = 3
CONV_WIDTH = 31
PAGE_SIZE = 128
EPS = 1e-6
IN_SIZES = (CONV_CH, CONV_CH, CONV_CH, ATTN_WIDTH, ATTN_WIDTH, ATTN_WIDTH, ATTN_WIDTH, D_MODEL, D_MODEL)
IN_COLS = sum(IN_SIZES)
NEG = -1e30

ROW_TILE = 256
OUT_ROW_TILE = 512
CONV_TILE = 256
CONV_SUB = 32
HIST_PAD = 32
PROMPT_HEADS_PER_STEP = 4
SAMPLE_PAGES_PER_STEP = 8
SAMPLE_RING_SLOTS = 4
VMEM_LIMIT = 56 * 1024 * 1024


def _silu(x):
    return x * jax.nn.sigmoid(x)


def _dot(a, b):
    return jnp.dot(a, b, preferred_element_type=F32)


def _dot_nt(a, b):
    return lax.dot_general(a, b, (((1,), (1,)), ((), ())), preferred_element_type=F32)


def _dot_nt_split(a, b):
    a_hi, b_hi = a.astype(BF16), b.astype(BF16)
    a_lo = (a - a_hi.astype(F32)).astype(BF16)
    b_lo = (b - b_hi.astype(F32)).astype(BF16)
    return _dot_nt(a_hi, b_hi) + (_dot_nt(a_hi, b_lo) + _dot_nt(a_lo, b_hi))


def _ada_body(c_ref, w_ref, b_ref, o_ref):
    a = _silu(c_ref[...]).astype(BF16)
    o_ref[...] = _dot(a, w_ref[...].astype(BF16)) + b_ref[...]


def _ada(c_all, w_ada, b_ada):
    n = c_all.shape[0]
    return pl.pallas_call(
        _ada_body,
        grid=(3,),
        in_specs=[pl.BlockSpec((n, D_MODEL), lambda j: (0, 0)),
                  pl.BlockSpec((D_MODEL, D_MODEL), lambda j: (0, j)),
                  pl.BlockSpec((1, D_MODEL), lambda j: (0, j))],
        out_specs=pl.BlockSpec((n, D_MODEL), lambda j: (0, j)),
        out_shape=jax.ShapeDtypeStruct((n, 3 * D_MODEL), F32),
        compiler_params=pltpu.CompilerParams(dimension_semantics=("arbitrary",),
                                             vmem_limit_bytes=VMEM_LIMIT),
        name="ada",
    )(c_all, w_ada, b_ada.reshape(1, 3 * D_MODEL))


def _rope_rows(x, cos, sin_lo, sin_hi):
    n = x.shape[-1]
    half = ROT_DIM // 2
    return x * cos + pltpu.roll(x, n - half, 1) * sin_lo + pltpu.roll(x, half, 1) * sin_hi


def _mixer_in_body(x_ref, scale_ref, shift_ref, g_ref, w_ref, cos_ref, slo_ref, shi_ref,
                   glu_ref, cg_ref, q_ref, k_ref, v_ref, ag_ref, gmc_ref, gma_ref, *, paged_kv):
    gb, tb, d = x_ref.shape
    m = gb * tb
    x = x_ref[...]
    r = x * lax.rsqrt(jnp.mean(x * x, axis=-1, keepdims=True) + EPS)
    h = (r * g_ref[...]) * (1.0 + scale_ref[...]) + shift_ref[...]
    hb = h.reshape(m, d).astype(BF16)

    def seg(idx):
        lo = sum(IN_SIZES[:idx])
        return _dot(hb, w_ref[:, lo:lo + IN_SIZES[idx]])

    def put(ref, val):
        ref[...] = val.reshape(ref.shape)

    def put_kv(ref, val):
        if not paged_kv:
            return put(ref, val)
        val_t = val.T
        for pg in range(m // PAGE_SIZE):
            ref[0, pg] = val_t[:, pg * PAGE_SIZE:(pg + 1) * PAGE_SIZE].reshape(N_HEADS, HEAD_DIM, PAGE_SIZE)

    put(glu_ref, seg(0) * jax.nn.sigmoid(seg(1)))
    put(cg_ref, seg(2))
    cos, slo, shi = cos_ref[...], slo_ref[...], shi_ref[...]
    put(q_ref, _rope_rows(seg(3), cos, slo, shi))
    put_kv(k_ref, _rope_rows(seg(4), cos, slo, shi))
    put_kv(v_ref, seg(5))
    put(ag_ref, seg(6))
    put(gmc_ref, seg(7))
    put(gma_ref, seg(8))


def _mixer_in(x3, scale, shift, norm_g, w_in_bf, tables, gb, tb, paged_kv):
    g, t, d = x3.shape
    m = gb * tb
    grid = (g // gb, t // tb)
    xmap = lambda i, j: (i, j, 0)
    bmap = lambda i, j: (i, 0, 0)
    cmap = lambda i, j: (0, 0)
    tmap = lambda i, j: (j, 0)
    widths = (CONV_CH, CONV_CH, ATTN_WIDTH, ATTN_WIDTH, ATTN_WIDTH, ATTN_WIDTH, D_MODEL, D_MODEL)
    out_specs = [pl.BlockSpec((gb, tb, w), xmap) for w in widths]
    out_shape = [jax.ShapeDtypeStruct((g, t, w), F32) for w in widths]
    if paged_kv:
        assert gb == 1 and tb % PAGE_SIZE == 0
        for idx in (3, 4):
            out_specs[idx] = pl.BlockSpec((1, tb // PAGE_SIZE, N_HEADS, HEAD_DIM, PAGE_SIZE),
                                          lambda i, j: (i, j, 0, 0, 0))
            out_shape[idx] = jax.ShapeDtypeStruct((g, t // PAGE_SIZE, N_HEADS, HEAD_DIM, PAGE_SIZE), F32)
    return pl.pallas_call(
        functools.partial(_mixer_in_body, paged_kv=paged_kv),
        grid=grid,
        in_specs=[pl.BlockSpec((gb, tb, d), xmap),
                  pl.BlockSpec((gb, 1, d), bmap),
                  pl.BlockSpec((gb, 1, d), bmap),
                  pl.BlockSpec((1, d), cmap),
                  pl.BlockSpec((d, IN_COLS), cmap, pipeline_mode=pl.Buffered(1)),
                  pl.BlockSpec((m, ATTN_WIDTH), tmap),
                  pl.BlockSpec((m, ATTN_WIDTH), tmap),
                  pl.BlockSpec((m, ATTN_WIDTH), tmap)],
        out_specs=out_specs,
        out_shape=out_shape,
        compiler_params=pltpu.CompilerParams(dimension_semantics=("parallel", "parallel"),
                                             vmem_limit_bytes=VMEM_LIMIT),
        name="mixer_in",
    )(x3, scale, shift, norm_g, w_in_bf, *tables)


def _rope_tables(pos):
    inv = ROPE_THETA ** (-(jnp.arange(0, ROT_DIM, 2, dtype=F32) / ROT_DIM))
    ang = pos.astype(F32)[:, None] * inv[None, :]
    cos, sin = jnp.cos(ang), jnp.sin(ang)
    half = ROT_DIM // 2
    n = pos.shape[0]
    pad = jnp.zeros((n, HEAD_DIM - ROT_DIM), F32)
    zero = jnp.zeros((n, half), F32)
    cos_h = jnp.concatenate([cos, cos, pad + 1.0], axis=1)
    slo_h = jnp.concatenate([-sin, zero, pad], axis=1)
    shi_h = jnp.concatenate([zero, sin, pad], axis=1)
    return tuple(jnp.tile(a, (1, N_HEADS)) for a in (cos_h, slo_h, shi_h))


def _conv_epilogue(y, cg, b, lg, lb):
    y = y + b
    mu = jnp.mean(y, axis=-1, keepdims=True)
    yc = y - mu
    var = jnp.mean(yc * yc, axis=-1, keepdims=True)
    z = _silu(yc * lax.rsqrt(var + EPS) * lg + lb)
    return z * _silu(cg)


def _conv_tap_offsets():
    off = HIST_PAD - (CONV_WIDTH - 1)
    return [divmod(off + tap, 8) for tap in range(CONV_WIDTH)]


def _conv_shift_span(tc):
    return tc + 8 * max(a for a, r in _conv_tap_offsets() if r)


def _conv_prompt_body(cur_ref, prev_ref, cg_ref, w_ref, b_ref, lg_ref, lb_ref, o_ref, hist_scr, shift_scr):
    j = pl.program_id(1)
    tc = cur_ref.shape[1]
    hist_scr[0:HIST_PAD, :] = jnp.where(j > 0, prev_ref[0], 0.0)
    hist_scr[HIST_PAD:HIST_PAD + tc, :] = cur_ref[0]
    span = _conv_shift_span(tc)
    for r in range(1, 8):
        shift_scr[r - 1] = hist_scr[r:r + span, :]
    for c in range(tc // CONV_SUB):
        r0 = c * CONV_SUB
        acc = jnp.zeros((CONV_SUB, CONV_CH), F32)
        for tap, (a, r) in enumerate(_conv_tap_offsets()):
            lo = r0 + 8 * a
            win = hist_scr[lo:lo + CONV_SUB, :] if r == 0 else shift_scr[r - 1, lo:lo + CONV_SUB, :]
            acc = acc + w_ref[tap:tap + 1, :] * win
        o_ref[0, r0:r0 + CONV_SUB, :] = _conv_epilogue(
            acc, cg_ref[0, r0:r0 + CONV_SUB, :], b_ref[...], lg_ref[...], lb_ref[...])


def _conv_prompt(glu, cg, conv_w, conv_b, ln_g, ln_b):
    b, s, c = glu.shape
    tc = CONV_TILE
    per = tc // HIST_PAD
    cmap = lambda i, j: (0, 0)
    return pl.pallas_call(
        _conv_prompt_body,
        grid=(b, s // tc),
        in_specs=[pl.BlockSpec((1, tc, c), lambda i, j: (i, j, 0)),
                  pl.BlockSpec((1, HIST_PAD, c), lambda i, j: (i, jnp.maximum(j * per - 1, 0), 0)),
                  pl.BlockSpec((1, tc, c), lambda i, j: (i, j, 0)),
                  pl.BlockSpec((CONV_WIDTH, c), cmap),
                  pl.BlockSpec((1, c), cmap),
                  pl.BlockSpec((1, c), cmap),
                  pl.BlockSpec((1, c), cmap)],
        out_specs=pl.BlockSpec((1, tc, c), lambda i, j: (i, j, 0)),
        out_shape=jax.ShapeDtypeStruct((b, s, c), F32),
        scratch_shapes=[pltpu.VMEM((HIST_PAD + tc, c), F32),
                        pltpu.VMEM((7, _conv_shift_span(tc), c), F32)],
        compiler_params=pltpu.CompilerParams(dimension_semantics=("parallel", "parallel"),
                                             vmem_limit_bytes=VMEM_LIMIT),
        name="conv_prompt",
    )(glu, glu, cg, conv_w, conv_b, ln_g, ln_b)


def _conv_sample_body(hist_ref, cg_ref, w_ref, b_ref, lg_ref, lb_ref, o_ref):
    nb, t, c = cg_ref.shape
    acc = jnp.zeros((nb, t, c), F32)
    for tap in range(CONV_WIDTH):
        acc = acc + w_ref[tap:tap + 1, :] * hist_ref[:, tap:tap + t, :]
    o_ref[...] = _conv_epilogue(acc, cg_ref[...], b_ref[...], lg_ref[...], lb_ref[...])


def _conv_sample(hist, cg, conv_w, conv_b, ln_g, ln_b):
    n, t, c = cg.shape
    nb = 8
    cmap = lambda i: (0, 0)
    return pl.pallas_call(
        _conv_sample_body,
        grid=(n // nb,),
        in_specs=[pl.BlockSpec((nb, hist.shape[1], c), lambda i: (i, 0, 0)),
                  pl.BlockSpec((nb, t, c), lambda i: (i, 0, 0)),
                  pl.BlockSpec((CONV_WIDTH, c), cmap),
                  pl.BlockSpec((1, c), cmap),
                  pl.BlockSpec((1, c), cmap),
                  pl.BlockSpec((1, c), cmap)],
        out_specs=pl.BlockSpec((nb, t, c), lambda i: (i, 0, 0)),
        out_shape=jax.ShapeDtypeStruct((n, t, c), F32),
        compiler_params=pltpu.CompilerParams(dimension_semantics=("parallel",),
                                             vmem_limit_bytes=VMEM_LIMIT),
        name="conv_sample",
    )(hist, cg, conv_w, conv_b, ln_g, ln_b)


def _prompt_attention_stages(i, q_ref, k_ref, v_ref, o_ref, kb_scr, vt_scr, kmean_scr, bias_scr, s_scr, acc_scr,
                             pmx_scr):
    tq = q_ref.shape[1]
    n_pages, heads = k_ref.shape[1], k_ref.shape[2]
    ppb = MOBA_BLOCK // PAGE_SIZE
    nblk = n_pages // ppb

    def fold(x, op):
        return op(x.reshape(MOBA_BLOCK // 8, 8, tq), axis=0)

    def prepare():
        for blk in range(nblk):
            ksum = jnp.zeros((1, heads * HEAD_DIM), F32)
            for hp in range(ppb):
                pg = blk * ppb + hp
                k_rows = k_ref[0, pg].reshape(heads * HEAD_DIM, PAGE_SIZE).T
                ksum = ksum + jnp.sum(k_rows, axis=0, keepdims=True)
                for hh in range(heads):
                    kb_scr[hh, pg * PAGE_SIZE:(pg + 1) * PAGE_SIZE, :] = (
                        k_rows[:, hh * HEAD_DIM:(hh + 1) * HEAD_DIM].astype(BF16))
            kmean_scr[blk:blk + 1, :] = ksum * (1.0 / MOBA_BLOCK)
            for hh in range(heads):
                vt_scr[hh, blk] = jnp.concatenate(
                    [v_ref[0, blk * ppb + hp, hh] for hp in range(ppb)], axis=1).astype(BF16)

    def stage1():
        pl.when(i == 0)(prepare)
        kmean = kmean_scr[...]
        blk_iota = lax.broadcasted_iota(jnp.int32, (nblk, tq), 0)
        past = blk_iota < i
        causal = (lax.broadcasted_iota(jnp.int32, (MOBA_BLOCK, tq), 0)
                  <= lax.broadcasted_iota(jnp.int32, (MOBA_BLOCK, tq), 1))
        qbs = []
        for hh in range(heads):
            q_h = q_ref[0, :, hh * HEAD_DIM:(hh + 1) * HEAD_DIM]
            s_blk = _dot_nt_split(kmean[:, hh * HEAD_DIM:(hh + 1) * HEAD_DIM], q_h)
            s_blk = jnp.where(past, s_blk, -jnp.inf)
            rank = jnp.zeros((nblk, tq), jnp.int32)
            for mm in range(nblk):
                sm = s_blk[mm:mm + 1, :]
                rank = rank + jnp.where(sm > s_blk, 1, jnp.where((sm == s_blk) & (blk_iota > mm), 1, 0))
            bias_scr[hh] = jnp.where(past & (rank < MOBA_TOPK), 0.0, NEG)
            qbs.append((q_h * (HEAD_DIM ** -0.5)).astype(BF16))

        def scores(hh, j):
            kj = kb_scr[hh, pl.ds(pl.multiple_of(j * MOBA_BLOCK, MOBA_BLOCK), MOBA_BLOCK), :]
            return _dot_nt(kj, qbs[hh])

        mx_own = []
        for hh in range(heads):
            s_t = jnp.where(causal, scores(hh, i), NEG)
            s_scr[hh, i] = s_t
            mx_own.append(fold(s_t, jnp.max))

        def score_blocks(jj, mx):
            out = list(mx)
            for j in (2 * jj, jnp.minimum(2 * jj + 1, i - 1)):
                for hh in range(heads):
                    s_t = scores(hh, j) + bias_scr[hh, pl.ds(j, 1), :]
                    s_scr[hh, j] = s_t
                    out[hh] = jnp.maximum(out[hh], fold(s_t, jnp.max))
            return tuple(out)

        mx = lax.fori_loop(0, (i + 1) // 2, score_blocks, tuple(mx_own))
        for hh in range(heads):
            pmx_scr[hh] = mx[hh]

    def stage2():
        m_fin = [jnp.max(pmx_scr[hh], axis=0, keepdims=True) for hh in range(heads)]
        for hh in range(heads):
            acc_scr[hh] = jnp.zeros((HEAD_DIM, tq), F32)

        def apply_blocks(jj, ls):
            out = list(ls)
            for j, live in ((2 * jj, True), (jnp.minimum(2 * jj + 1, i), 2 * jj + 1 <= i)):
                for hh in range(heads):
                    shift = m_fin[hh] if live is True else jnp.where(live, m_fin[hh], jnp.inf)
                    p = jnp.exp(s_scr[hh, j] - shift)
                    out[hh] = out[hh] + fold(p, jnp.sum)
                    acc_scr[hh] = acc_scr[hh] + _dot(vt_scr[hh, j], p.astype(BF16))
            return tuple(out)

        ls = lax.fori_loop(0, i // 2 + 1, apply_blocks, tuple(jnp.zeros((8, tq), F32) for _ in range(heads)))
        o_ref[0] = jnp.concatenate(
            [acc_scr[hh] / jnp.sum(ls[hh], axis=0, keepdims=True) for hh in range(heads)], axis=0).T

    return stage1, stage2


def _prompt_scratch(s, tq, heads):
    nblk = s // MOBA_BLOCK
    return [pltpu.VMEM((heads, s, HEAD_DIM), BF16),
            pltpu.VMEM((heads, nblk, HEAD_DIM, MOBA_BLOCK), BF16),
            pltpu.VMEM((nblk, heads * HEAD_DIM), F32),
            pltpu.VMEM((heads, nblk, tq), F32),
            pltpu.VMEM((heads, nblk, MOBA_BLOCK, tq), F32),
            pltpu.VMEM((heads, HEAD_DIM, tq), F32),
            pltpu.VMEM((heads, 8, tq), F32)]


def _sample_stream(slot, pt_ref, q_ref, kn_ref, vn_ref, ck_hbm, cv_hbm, o_ref,
                   kbuf, vbuf, ksem, vsem, qbd_scr, s_scr, mx_scr, sb_scr, p_scr, pown_scr, l_scr, acc_scr,
                   *, n_seq, n_chunks, pages_per_step):
    ds = q_ref.shape[1]
    rows = N_HEADS * ds
    ppb = MOBA_BLOCK // PAGE_SIZE
    n_pages = n_chunks * pages_per_step
    n_blocks = n_pages // ppb
    n_global = (n_seq + 1) * n_chunks
    lookahead = SAMPLE_RING_SLOTS - 1
    lane = lax.broadcasted_iota(jnp.int32, (rows, 128), 1)
    row = lax.broadcasted_iota(jnp.int32, (rows, 128), 0)

    def page_copies(g, start):
        g_slot = g // n_chunks
        g_c = g % n_chunks
        ring = g % SAMPLE_RING_SLOTS

        def each(cache, buf, sem, seq):
            for p in range(pages_per_step):
                if start:
                    page = pt_ref[seq * n_pages + g_c * pages_per_step + p]
                    pltpu.make_async_copy(cache.at[page], buf.at[ring, p], sem.at[ring]).start()
                else:
                    pltpu.make_async_copy(cache.at[0], buf.at[ring, p], sem.at[ring]).wait()

        @pl.when(g_slot < n_seq)
        def _():
            each(ck_hbm, kbuf, ksem, g_slot)

        @pl.when(g_slot > 0)
        def _():
            each(cv_hbm, vbuf, vsem, g_slot - 1)

    def prime():
        for g in range(lookahead):
            page_copies(jnp.int32(g), start=True)

    def scaled_q():
        return (qbd_scr[...] * (HEAD_DIM ** -0.5)).astype(BF16)

    def finalize():
        sc = jnp.where(lane < n_blocks, sb_scr[...], -jnp.inf)
        lane_f = lane.astype(F32)
        sel = jnp.zeros((rows, 128), F32)
        for _ in range(min(MOBA_TOPK, n_blocks)):
            top = jnp.max(sc, axis=1, keepdims=True)
            first = jnp.min(jnp.where(sc == top, lane_f, 1e9), axis=1, keepdims=True)
            pick = lane_f == first
            sel = jnp.where(pick, 1.0, sel)
            sc = jnp.where(pick, -jnp.inf, sc)
        kn = jnp.concatenate([kn_ref[0], jnp.zeros((128 - ds, ATTN_WIDTH), F32)], axis=0)
        s_own = jnp.where(lane <= row % ds, _dot_nt(scaled_q(), kn.astype(BF16)), NEG)
        m_sel = jnp.max(jnp.where(sel > 0, mx_scr[...], NEG), axis=1, keepdims=True)
        m_fin = jnp.maximum(m_sel, jnp.max(s_own, axis=1, keepdims=True))
        p_own = jnp.exp(s_own - m_fin)
        pown_scr[...] = p_own
        p_sum = p_own
        for blk in range(n_blocks):
            chosen = sel[:, blk:blk + 1] > 0
            p = jnp.where(chosen, jnp.exp(s_scr[blk] - m_fin), 0.0)
            p_scr[blk] = p.astype(BF16)
            for hp in range(ppb):
                p_sum = p_sum + p[:, hp * PAGE_SIZE:(hp + 1) * PAGE_SIZE]
        l_scr[...] = jnp.broadcast_to(jnp.sum(p_sum, axis=1, keepdims=True), l_scr.shape)
        acc_scr[...] = jnp.zeros(acc_scr.shape, F32)

    def load_queries():
        q = q_ref[0]
        r_i = lax.broadcasted_iota(jnp.int32, (rows, ATTN_WIDTH), 0)
        l_i = lax.broadcasted_iota(jnp.int32, (rows, ATTN_WIDTH), 1)
        qbd_scr[...] = jnp.where(r_i // ds == l_i // HEAD_DIM, jnp.concatenate([q] * N_HEADS, axis=0), 0.0)
        mx_scr[...] = jnp.full(mx_scr.shape, NEG, F32)
        sb_scr[...] = jnp.zeros(sb_scr.shape, F32)

    def block_operand(buf, ring, bb):
        return jnp.concatenate(
            [buf[ring, bb * ppb + hp].reshape(ATTN_WIDTH, PAGE_SIZE).astype(BF16) for hp in range(ppb)], axis=1)

    def score_k_pages(c, ring):
        qb = scaled_q()
        mx = mx_scr[...]
        sb = sb_scr[...]
        for bb in range(pages_per_step // ppb):
            blk = c * (pages_per_step // ppb) + bb
            sc = _dot(qb, block_operand(kbuf, ring, bb))
            s_scr[blk] = sc
            mx = jnp.where(lane == blk, jnp.max(sc, axis=1, keepdims=True), mx)
            sb = jnp.where(lane == blk, jnp.sum(sc, axis=1, keepdims=True), sb)
        mx_scr[...] = mx
        sb_scr[...] = sb

    def apply_v_pages(c, ring):
        acc = acc_scr[...]
        for bb in range(pages_per_step // ppb):
            p = p_scr[c * (pages_per_step // ppb) + bb]
            acc = acc + _dot_nt(p, block_operand(vbuf, ring, bb))
        acc_scr[...] = acc

    def chunk(c, carry):
        g = slot * n_chunks + c
        ring = g % SAMPLE_RING_SLOTS

        @pl.when(g + lookahead < n_global)
        def _():
            page_copies(g + lookahead, start=True)

        page_copies(g, start=False)

        @pl.when((slot > 0) & (slot < n_seq))
        def _():
            score_k_pages(c, ring)
            apply_v_pages(c, ring)

        @pl.when(slot == 0)
        def _():
            score_k_pages(c, ring)

        @pl.when(slot == n_seq)
        def _():
            apply_v_pages(c, ring)

        return carry

    def write_output():
        vn = jnp.concatenate([vn_ref[0], jnp.zeros((128 - ds, ATTN_WIDTH), F32)], axis=0)
        full = acc_scr[...] + _dot(pown_scr[...].astype(BF16), vn.astype(BF16))
        full = full / l_scr[:, 0:1]
        l_i = lax.broadcasted_iota(jnp.int32, (ds, ATTN_WIDTH), 1)
        out = jnp.zeros((ds, ATTN_WIDTH), F32)
        for hh in range(N_HEADS):
            out = out + jnp.where(l_i // HEAD_DIM == hh, full[hh * ds:(hh + 1) * ds, :], 0.0)
        o_ref[0] = out

    def begin():
        pl.when(slot == 0)(prime)
        pl.when(slot > 0)(finalize)
        pl.when(slot < n_seq)(load_queries)

    def run_chunks(c0, c1):
        lax.fori_loop(c0, c1, chunk, 0)

    def end():
        pl.when(slot > 0)(write_output)

    return begin, run_chunks, end


def _sample_scratch(ds, n_pages, pps):
    rows = N_HEADS * ds
    n_blocks = n_pages * PAGE_SIZE // MOBA_BLOCK
    ring_buf = pltpu.VMEM((SAMPLE_RING_SLOTS, pps, N_HEADS, HEAD_DIM, PAGE_SIZE), F32)
    return [ring_buf, ring_buf,
            pltpu.SemaphoreType.DMA((SAMPLE_RING_SLOTS,)),
            pltpu.SemaphoreType.DMA((SAMPLE_RING_SLOTS,)),
            pltpu.VMEM((rows, ATTN_WIDTH), F32),
            pltpu.VMEM((n_blocks, rows, MOBA_BLOCK), F32),
            pltpu.VMEM((rows, 128), F32),
            pltpu.VMEM((rows, 128), F32),
            pltpu.VMEM((n_blocks, rows, MOBA_BLOCK), BF16),
            pltpu.VMEM((rows, 128), F32),
            pltpu.VMEM((rows, 128), F32),
            pltpu.VMEM((rows, ATTN_WIDTH), F32)]


def _moba_body(pt_ref, qs_ref, kn_ref, vn_ref, ck_hbm, cv_hbm, qp_ref, kp_ref, vp_ref, os_ref, op_ref, *scratch,
               n_seq, n_chunks, pages_per_step, n_prompt_steps, n_qtiles, n_sample_scratch):
    t = pl.program_id(0)
    begin, run_chunks, end = _sample_stream(
        t, pt_ref, qs_ref, kn_ref, vn_ref, ck_hbm, cv_hbm, os_ref, *scratch[:n_sample_scratch],
        n_seq=n_seq, n_chunks=n_chunks, pages_per_step=pages_per_step)
    stage1, stage2 = _prompt_attention_stages(t % n_qtiles, qp_ref, kp_ref, vp_ref, op_ref,
                                              *scratch[n_sample_scratch:])
    n_steps = max(n_seq + 1, n_prompt_steps)
    sample_part = (lambda f: f()) if n_steps == n_seq + 1 else (lambda f: pl.when(t <= n_seq)(f))
    prompt_part = (lambda f: f()) if n_steps == n_prompt_steps else (lambda f: pl.when(t < n_prompt_steps)(f))
    c1, c2 = n_chunks // 4, (5 * n_chunks) // 8

    def sample_head():
        begin()
        run_chunks(0, c1)

    def sample_middle():
        run_chunks(c1, c2)

    def sample_tail():
        run_chunks(c2, n_chunks)
        end()

    sample_part(sample_head)
    prompt_part(stage1)
    sample_part(sample_middle)
    prompt_part(stage2)
    sample_part(sample_tail)


def _moba(q_p, k_paged, v_paged, q_s, k_new, v_new, cache_k, cache_v, page_table):
    b, s, w = q_p.shape
    db, ds, _ = q_s.shape
    n_pages = page_table.shape[1]
    assert (n_pages * PAGE_SIZE) % MOBA_BLOCK == 0, "past length must be whole MoBA blocks"
    pps = SAMPLE_PAGES_PER_STEP
    n_chunks = n_pages // pps
    assert n_chunks * pps == n_pages and pps % (MOBA_BLOCK // PAGE_SIZE) == 0
    ck = jnp.transpose(cache_k, (0, 2, 3, 1))
    cv = jnp.transpose(cache_v, (0, 2, 3, 1))
    pt = page_table.reshape(-1)

    tq = MOBA_BLOCK
    heads = PROMPT_HEADS_PER_STEP
    lanes = heads * HEAD_DIM
    n_qt, n_hg = s // tq, w // lanes
    n_prompt = b * n_hg * n_qt
    n_steps = max(db + 1, n_prompt)

    def prompt_index(t):
        tp = jnp.minimum(t, n_prompt - 1)
        return tp // (n_hg * n_qt), (tp // n_qt) % n_hg, tp % n_qt

    def q_map(t, pt_ref):
        bi, hg, i = prompt_index(t)
        return (bi, i, hg)

    def kv_map(t, pt_ref):
        bi, hg, _ = prompt_index(t)
        return (bi, 0, hg, 0, 0)

    cur = pl.BlockSpec((1, ds, w), lambda t, pt_ref: (jnp.minimum(t, db - 1), 0, 0))
    prev = pl.BlockSpec((1, ds, w), lambda t, pt_ref: (jnp.clip(t - 1, 0, db - 1), 0, 0))
    hbm = pl.BlockSpec(memory_space=pl.ANY)
    q_spec = pl.BlockSpec((1, tq, lanes), q_map)
    kv_spec = pl.BlockSpec((1, s // PAGE_SIZE, heads, HEAD_DIM, PAGE_SIZE), kv_map)
    sample_scratch = _sample_scratch(ds, n_pages, pps)
    grid_spec = pltpu.PrefetchScalarGridSpec(
        num_scalar_prefetch=1,
        grid=(n_steps,),
        in_specs=[cur, prev, prev, hbm, hbm, q_spec, kv_spec, kv_spec],
        out_specs=[prev, q_spec],
        scratch_shapes=sample_scratch + _prompt_scratch(s, tq, heads))
    body = functools.partial(_moba_body, n_seq=db, n_chunks=n_chunks, pages_per_step=pps,
                             n_prompt_steps=n_prompt, n_qtiles=n_qt, n_sample_scratch=len(sample_scratch))
    y_s, y_p = pl.pallas_call(
        body,
        grid_spec=grid_spec,
        out_shape=[jax.ShapeDtypeStruct((db, ds, w), F32), jax.ShapeDtypeStruct((b, s, w), F32)],
        compiler_params=pltpu.CompilerParams(dimension_semantics=("arbitrary",),
                                             vmem_limit_bytes=VMEM_LIMIT),
        name="moba",
    )(pt, q_s, k_new, v_new, ck, cv, q_p, k_paged, v_paged)
    return y_p, y_s


def _mixer_out_body(x_ref, yc_ref, ya_ref, ag_ref, gmc_ref, gma_ref, gate_ref, wb_ref, wo_ref, fg_ref, o_ref):
    gb, tb, d = x_ref.shape
    m = gb * tb
    yc = yc_ref[...].reshape(m, CONV_CH).astype(BF16)
    ya = (ya_ref[...] * _silu(ag_ref[...])).reshape(m, ATTN_WIDTH).astype(BF16)
    merged = (jax.nn.sigmoid(gmc_ref[...].reshape(m, d)) * _dot(yc, wb_ref[0])
              + jax.nn.sigmoid(gma_ref[...].reshape(m, d)) * _dot(ya, wb_ref[1]))
    o = _dot(merged.astype(BF16), wo_ref[...]).reshape(gb, tb, d)
    xo = x_ref[...] + gate_ref[...] * o
    r = xo * lax.rsqrt(jnp.mean(xo * xo, axis=-1, keepdims=True) + EPS)
    o_ref[...] = r * fg_ref[...]


def _mixer_out(x3, y_conv, y_attn, ag, gmc, gma, gate, wb_bf, wo_bf, final_g, gb, tb):
    g, t, d = x3.shape
    xmap = lambda i, j: (i, j, 0)
    bmap = lambda i, j: (i, 0, 0)
    spec = lambda w: pl.BlockSpec((gb, tb, w), xmap)
    return pl.pallas_call(
        _mixer_out_body,
        grid=(g // gb, t // tb),
        in_specs=[spec(d), spec(CONV_CH), spec(ATTN_WIDTH), spec(ATTN_WIDTH), spec(d), spec(d),
                  pl.BlockSpec((gb, 1, d), bmap),
                  pl.BlockSpec((2, CONV_CH, d), lambda i, j: (0, 0, 0)),
                  pl.BlockSpec((d, d), lambda i, j: (0, 0)),
                  pl.BlockSpec((1, d), lambda i, j: (0, 0))],
        out_specs=spec(d),
        out_shape=jax.ShapeDtypeStruct((g, t, d), F32),
        compiler_params=pltpu.CompilerParams(dimension_semantics=("parallel", "parallel"),
                                             vmem_limit_bytes=VMEM_LIMIT),
        name="mixer_out",
    )(x3, y_conv, y_attn, ag, gmc, gma, gate, wb_bf, wo_bf, final_g)


def kernel(x_prompt, x_sample, cache_k, cache_v, state_conv, page_table, c_prompt, c_sample, norm_g, w_ada,
           b_ada, w_in, conv_w, conv_b, conv_ln_g, conv_ln_b, w_branch, w_out, final_g):
    depth = norm_g.shape[0]
    assert depth == 1, "single-layer trunk"
    b, s, d = x_prompt.shape
    db, ds, _ = x_sample.shape
    assert CONV_CH == ATTN_WIDTH and s % MOBA_BLOCK == 0 and s >= CONV_WIDTH - 1
    past_len = page_table.shape[1] * PAGE_SIZE
    lyr = 0

    ada = _ada(jnp.concatenate([c_prompt, c_sample], axis=0), w_ada[lyr], b_ada[lyr])
    shift, scale, gate = (ada[:, j * d:(j + 1) * d][:, None, :] for j in range(3))
    w_in_bf = w_in[lyr].astype(BF16)
    wb_bf = w_branch[lyr].astype(BF16)
    wo_bf = w_out[lyr].astype(BF16)
    g_in = norm_g[lyr].reshape(1, d)
    g_fin = final_g.reshape(1, d)
    cw, cb = conv_w[lyr], conv_b[lyr].reshape(1, CONV_CH)
    lg, lb = conv_ln_g[lyr].reshape(1, CONV_CH), conv_ln_b[lyr].reshape(1, CONV_CH)

    tab_p = _rope_tables(jnp.arange(s))
    glu_p, cg_p, q_p, k_p, v_p, ag_p, gmc_p, gma_p = _mixer_in(
        x_prompt, scale[:b], shift[:b], g_in, w_in_bf, tab_p, 1, ROW_TILE, paged_kv=True)
    gbs = ROW_TILE // ds
    tab_s = tuple(jnp.tile(a, (gbs, 1)) for a in _rope_tables(past_len + jnp.arange(ds)))
    glu_s, cg_s, q_s, k_s, v_s, ag_s, gmc_s, gma_s = _mixer_in(
        x_sample, scale[b:], shift[b:], g_in, w_in_bf, tab_s, gbs, ds, paged_kv=False)
    yc_p = _conv_prompt(glu_p, cg_p, cw, cb, lg, lb)
    hist = jnp.concatenate([state_conv[lyr], glu_s], axis=1)
    yc_s = _conv_sample(hist, cg_s, cw, cb, lg, lb)

    ya_p, ya_s = _moba(q_p, k_p, v_p, q_s, k_s, v_s, cache_k[lyr], cache_v[lyr], page_table)

    y_prompt = _mixer_out(x_prompt, yc_p, ya_p, ag_p, gmc_p, gma_p, gate[:b], wb_bf, wo_bf, g_fin, 1, OUT_ROW_TILE)
    y_sample = _mixer_out(x_sample, yc_s, ya_s, ag_s, gmc_s, gma_s, gate[b:], wb_bf, wo_bf, g_fin,
                          OUT_ROW_TILE // ds, ds)
    k_prompt = jnp.transpose(k_p, (0, 1, 4, 2, 3))[None]
    v_prompt = jnp.transpose(v_p, (0, 1, 4, 2, 3))[None]
    conv_prompt = glu_p[None, :, s - (CONV_WIDTH - 1):, :]
    k_sample = k_s.reshape(1, db, ds, N_HEADS, HEAD_DIM)
    v_sample = v_s.reshape(1, db, ds, N_HEADS, HEAD_DIM)
    conv_sample = hist[None, :, ds:, :]

    return (y_prompt, y_sample, k_prompt, v_prompt, conv_prompt, k_sample, v_sample, conv_sample)
```

```python
import functools

import jax
import jax.numpy as jnp
from jax import lax
from jax.experimental import pallas as pl
from jax.experimental.pallas import tpu as pltpu

F32 = jnp.float32
BF16 = jnp.bfloat16

D_MODEL = 1024
HEAD_DIM = 64
N_HEADS = 8
ATTN_WIDTH = N_HEADS * HEAD_DIM
CONV_CH = 512
ROT_DIM = HEAD_DIM // 4
ROPE_THETA = 500000.0
MOBA_BLOCK = 256
MOBA_TOPK = 3
CONV_WIDTH = 31
PAGE_SIZE = 128
EPS = 1e-6
IN_SIZES = (CONV_CH, CONV_CH, CONV_CH, ATTN_WIDTH, ATTN_WIDTH, ATTN_WIDTH, ATTN_WIDTH, D_MODEL, D_MODEL)
IN_COLS = sum(IN_SIZES)
NEG = -1e30

ROPE_TABLE_LANES = 128
ROW_TILE = 256
OUT_ROW_TILE = 512
CONV_SUB = 32
HIST_PAD = 32
PROMPT_HEADS_PER_STEP = 4
SAMPLE_PAGES_PER_STEP = 8
SAMPLE_RING_SLOTS = 4
VMEM_LIMIT = 56 * 1024 * 1024


def _silu(x):
    return x * jax.nn.sigmoid(x)


def _dot(a, b):
    return jnp.dot(a, b, preferred_element_type=F32)


def _dot_nt(a, b):
    return lax.dot_general(a, b, (((1,), (1,)), ((), ())), preferred_element_type=F32)


def _dot_nt_split(a, b):
    a_hi, b_hi = a.astype(BF16), b.astype(BF16)
    a_lo = (a - a_hi.astype(F32)).astype(BF16)
    b_lo = (b - b_hi.astype(F32)).astype(BF16)
    return _dot_nt(a_hi, b_hi) + (_dot_nt(a_hi, b_lo) + _dot_nt(a_lo, b_hi))


def _ada_body(c_ref, w_ref, b_ref, o_ref):
    a = _silu(c_ref[...]).astype(BF16)
    o_ref[...] = _dot(a, w_ref[...].astype(BF16)) + b_ref[...]


def _ada(c_all, w_ada, b_ada):
    n = c_all.shape[0]
    return pl.pallas_call(
        _ada_body,
        grid=(3,),
        in_specs=[pl.BlockSpec((n, D_MODEL), lambda j: (0, 0)),
                  pl.BlockSpec((D_MODEL, D_MODEL), lambda j: (0, j)),
                  pl.BlockSpec((1, D_MODEL), lambda j: (0, j))],
        out_specs=pl.BlockSpec((n, D_MODEL), lambda j: (0, j)),
        out_shape=jax.ShapeDtypeStruct((n, 3 * D_MODEL), F32),
        compiler_params=pltpu.CompilerParams(dimension_semantics=("arbitrary",),
                                             vmem_limit_bytes=VMEM_LIMIT),
        name="ada",
    )(c_all, w_ada, b_ada.reshape(1, 3 * D_MODEL))


def _rope_rows(x, cos, sin_lo, sin_hi):
    n = x.shape[-1]
    half = ROT_DIM // 2
    return x * cos + pltpu.roll(x, n - half, 1) * sin_lo + pltpu.roll(x, half, 1) * sin_hi


def _mixer_in_body(*refs, prompt):
    x_ref, scale_ref, shift_ref, g_ref, w_ref, cos_ref, slo_ref, shi_ref = refs[:8]
    if prompt:
        cw_ref, cb_ref, lg_ref, lb_ref = refs[8:12]
        yconv_ref, tail_ref, q_ref, k_ref, v_ref, ag_ref, gmc_ref, gma_ref, hist_scr, shift_scr = refs[12:]
    else:
        glu_ref, cg_ref, q_ref, k_ref, v_ref, ag_ref, gmc_ref, gma_ref = refs[8:]
    gb, tb, d = x_ref.shape
    m = gb * tb
    if prompt:
        _carry_conv_history(pl.program_id(1) == 0, hist_scr, tb)
    x = x_ref[...]
    r = x * lax.rsqrt(jnp.mean(x * x, axis=-1, keepdims=True) + EPS)
    h = (r * g_ref[...]) * (1.0 + scale_ref[...]) + shift_ref[...]
    hb = h.reshape(m, d).astype(BF16)
    starts = [sum(IN_SIZES[:idx]) for idx in range(len(IN_SIZES))]
    half = D_MODEL // 2

    def cols(lo, width):
        return _dot(hb, w_ref[:, lo:lo + width])

    def put(ref, val):
        ref[...] = val.reshape(ref.shape)

    def put_half(ref, lo, part):
        ref[:, :, part * half:(part + 1) * half] = cols(lo + part * half, half).reshape(gb, tb, half)

    def put_kv(ref, val):
        if not prompt:
            return put(ref, val)
        val_t = val.T
        for pg in range(m // PAGE_SIZE):
            ref[0, pg] = val_t[:, pg * PAGE_SIZE:(pg + 1) * PAGE_SIZE].reshape(N_HEADS, HEAD_DIM, PAGE_SIZE)

    def rope(val):
        wide = lambda ref: jnp.concatenate([ref[...]] * (ATTN_WIDTH // ROPE_TABLE_LANES), axis=1)
        return _rope_rows(val, wide(cos_ref), wide(slo_ref), wide(shi_ref))

    glu = cols(starts[0], CONV_CH) * jax.nn.sigmoid(cols(starts[1], CONV_CH))
    cg = cols(starts[2], CONV_CH)
    pieces = [lambda: put(q_ref, rope(cols(starts[3], ATTN_WIDTH))),
              lambda: put_kv(k_ref, rope(cols(starts[4], ATTN_WIDTH))),
              lambda: put_kv(v_ref, cols(starts[5], ATTN_WIDTH)),
              lambda: put(ag_ref, cols(starts[6], ATTN_WIDTH)),
              lambda: put_half(gmc_ref, starts[7], 0), lambda: put_half(gmc_ref, starts[7], 1),
              lambda: put_half(gma_ref, starts[8], 0), lambda: put_half(gma_ref, starts[8], 1)]
    if prompt:
        conv_chunks = _causal_conv_tile(glu, cg, cw_ref, cb_ref, lg_ref, lb_ref, yconv_ref, tail_ref,
                                        hist_scr, shift_scr)
    else:
        put(glu_ref, glu)
        put(cg_ref, cg)
        conv_chunks = []
    per_piece = -(-len(conv_chunks) // len(pieces))
    for idx, piece in enumerate(pieces):
        piece()
        for chunk in conv_chunks[idx * per_piece:(idx + 1) * per_piece]:
            chunk()


def _mixer_in(x3, scale, shift, norm_g, w_in_bf, tables, gb, tb, conv_params=None):
    g, t, d = x3.shape
    m = gb * tb
    prompt = conv_params is not None
    grid = (g // gb, t // tb)
    xmap = lambda i, j: (i, j, 0)
    bmap = lambda i, j: (i, 0, 0)
    cmap = lambda i, j: (0, 0)
    tmap = lambda i, j: (j, 0)
    widths = (CONV_CH, CONV_CH, ATTN_WIDTH, ATTN_WIDTH, ATTN_WIDTH, ATTN_WIDTH, D_MODEL, D_MODEL)
    out_specs = [pl.BlockSpec((gb, tb, w), xmap) for w in widths]
    out_shape = [jax.ShapeDtypeStruct((g, t, w), F32) for w in widths]
    in_specs = [pl.BlockSpec((gb, tb, d), xmap),
                pl.BlockSpec((gb, 1, d), bmap),
                pl.BlockSpec((gb, 1, d), bmap),
                pl.BlockSpec((1, d), cmap),
                pl.BlockSpec((d, IN_COLS), cmap, pipeline_mode=pl.Buffered(1)),
                pl.BlockSpec((m, ROPE_TABLE_LANES), tmap),
                pl.BlockSpec((m, ROPE_TABLE_LANES), tmap),
                pl.BlockSpec((m, ROPE_TABLE_LANES), tmap)]
    operands = [x3, scale, shift, norm_g, w_in_bf, *tables]
    scratch = []
    if prompt:
        assert gb == 1 and tb % PAGE_SIZE == 0 and tb >= HIST_PAD
        out_shape[0] = jax.ShapeDtypeStruct((g, t, CONV_CH), BF16)
        out_specs[1] = pl.BlockSpec((1, HIST_PAD, CONV_CH), bmap)
        out_shape[1] = jax.ShapeDtypeStruct((g, HIST_PAD, CONV_CH), F32)
        for idx in (3, 4):
            out_specs[idx] = pl.BlockSpec((1, tb // PAGE_SIZE, N_HEADS, HEAD_DIM, PAGE_SIZE),
                                          lambda i, j: (i, j, 0, 0, 0))
            out_shape[idx] = jax.ShapeDtypeStruct((g, t // PAGE_SIZE, N_HEADS, HEAD_DIM, PAGE_SIZE), F32)
        in_specs += [pl.BlockSpec((CONV_WIDTH, CONV_CH), cmap)] + [pl.BlockSpec((1, CONV_CH), cmap)] * 3
        operands += list(conv_params)
        scratch = [pltpu.VMEM((HIST_PAD + tb, CONV_CH), F32),
                   pltpu.VMEM((7, _conv_shift_span(tb), CONV_CH), F32)]
    return pl.pallas_call(
        functools.partial(_mixer_in_body, prompt=prompt),
        grid=grid,
        in_specs=in_specs,
        out_specs=out_specs,
        out_shape=out_shape,
        scratch_shapes=scratch,
        compiler_params=pltpu.CompilerParams(
            dimension_semantics=("parallel", "arbitrary" if prompt else "parallel"),
            vmem_limit_bytes=VMEM_LIMIT),
        name="mixer_in",
    )(*operands)


def _rope_tables(pos):
    inv = ROPE_THETA ** (-(jnp.arange(0, ROT_DIM, 2, dtype=F32) / ROT_DIM))
    ang = pos.astype(F32)[:, None] * inv[None, :]
    cos, sin = jnp.cos(ang), jnp.sin(ang)
    half = ROT_DIM // 2
    n = pos.shape[0]
    pad = jnp.zeros((n, HEAD_DIM - ROT_DIM), F32)
    zero = jnp.zeros((n, half), F32)
    cos_h = jnp.concatenate([cos, cos, pad + 1.0], axis=1)
    slo_h = jnp.concatenate([-sin, zero, pad], axis=1)
    shi_h = jnp.concatenate([zero, sin, pad], axis=1)
    return tuple(jnp.tile(a, (1, ROPE_TABLE_LANES // HEAD_DIM)) for a in (cos_h, slo_h, shi_h))


def _conv_epilogue(y, cg, b, lg, lb):
    y = y + b
    mu = jnp.mean(y, axis=-1, keepdims=True)
    yc = y - mu
    var = jnp.mean(yc * yc, axis=-1, keepdims=True)
    z = _silu(yc * lax.rsqrt(var + EPS) * lg + lb)
    return z * _silu(cg)


def _conv_tap_offsets():
    off = HIST_PAD - (CONV_WIDTH - 1)
    return [divmod(off + tap, 8) for tap in range(CONV_WIDTH)]


def _conv_shift_span(tc):
    return tc + 8 * max(a for a, r in _conv_tap_offsets() if r)


def _carry_conv_history(first, hist_scr, tc):
    @pl.when(first)
    def _():
        hist_scr[0:HIST_PAD, :] = jnp.zeros((HIST_PAD, CONV_CH), F32)

    @pl.when(jnp.logical_not(first))
    def _():
        hist_scr[0:HIST_PAD, :] = hist_scr[tc:tc + HIST_PAD, :]


def _causal_conv_tile(glu, cg, w_ref, b_ref, lg_ref, lb_ref, o_ref, tail_ref, hist_scr, shift_scr):
    tc = glu.shape[0]
    hist_scr[HIST_PAD:HIST_PAD + tc, :] = glu
    tail_ref[0] = glu[tc - HIST_PAD:, :]
    span = _conv_shift_span(tc)
    for r in range(1, 8):
        shift_scr[r - 1] = hist_scr[r:r + span, :]

    def rows(r0):
        acc = jnp.zeros((CONV_SUB, CONV_CH), F32)
        for tap, (a, r) in enumerate(_conv_tap_offsets()):
            lo = r0 + 8 * a
            win = hist_scr[lo:lo + CONV_SUB, :] if r == 0 else shift_scr[r - 1, lo:lo + CONV_SUB, :]
            acc = acc + w_ref[tap:tap + 1, :] * win
        o_ref[0, r0:r0 + CONV_SUB, :] = _conv_epilogue(
            acc, cg[r0:r0 + CONV_SUB, :], b_ref[...], lg_ref[...], lb_ref[...]).astype(o_ref.dtype)

    return [functools.partial(rows, c * CONV_SUB) for c in range(tc // CONV_SUB)]


def _conv_sample_body(hist_ref, cg_ref, w_ref, b_ref, lg_ref, lb_ref, o_ref):
    nb, t, c = cg_ref.shape
    acc = jnp.zeros((nb, t, c), F32)
    for tap in range(CONV_WIDTH):
        acc = acc + w_ref[tap:tap + 1, :] * hist_ref[:, tap:tap + t, :]
    o_ref[...] = _conv_epilogue(acc, cg_ref[...], b_ref[...], lg_ref[...], lb_ref[...])


def _conv_sample(hist, cg, conv_w, conv_b, ln_g, ln_b):
    n, t, c = cg.shape
    nb = 8
    cmap = lambda i: (0, 0)
    return pl.pallas_call(
        _conv_sample_body,
        grid=(n // nb,),
        in_specs=[pl.BlockSpec((nb, hist.shape[1], c), lambda i: (i, 0, 0)),
                  pl.BlockSpec((nb, t, c), lambda i: (i, 0, 0)),
                  pl.BlockSpec((CONV_WIDTH, c), cmap),
                  pl.BlockSpec((1, c), cmap),
                  pl.BlockSpec((1, c), cmap),
                  pl.BlockSpec((1, c), cmap)],
        out_specs=pl.BlockSpec((nb, t, c), lambda i: (i, 0, 0)),
        out_shape=jax.ShapeDtypeStruct((n, t, c), F32),
        compiler_params=pltpu.CompilerParams(dimension_semantics=("parallel",),
                                             vmem_limit_bytes=VMEM_LIMIT),
        name="conv_sample",
    )(hist, cg, conv_w, conv_b, ln_g, ln_b)


def _prompt_attention_stages(i, q_ref, k_ref, v_ref, o_ref, kb_scr, vt_scr, kmean_scr, bias_scr, s_scr, acc_scr,
                             pmx_scr):
    tq = q_ref.shape[1]
    n_pages, heads = k_ref.shape[1], k_ref.shape[2]
    ppb = MOBA_BLOCK // PAGE_SIZE
    nblk = n_pages // ppb

    def fold(x, op):
        return op(x.reshape(MOBA_BLOCK // 8, 8, tq), axis=0)

    def prepare():
        for blk in range(nblk):
            ksum = jnp.zeros((1, heads * HEAD_DIM), F32)
            for hp in range(ppb):
                pg = blk * ppb + hp
                k_rows = k_ref[0, pg].reshape(heads * HEAD_DIM, PAGE_SIZE).T
                ksum = ksum + jnp.sum(k_rows, axis=0, keepdims=True)
                for hh in range(heads):
                    kb_scr[hh, pg * PAGE_SIZE:(pg + 1) * PAGE_SIZE, :] = (
                        k_rows[:, hh * HEAD_DIM:(hh + 1) * HEAD_DIM].astype(BF16))
            kmean_scr[blk:blk + 1, :] = ksum * (1.0 / MOBA_BLOCK)
            for hh in range(heads):
                vt_scr[hh, blk] = jnp.concatenate(
                    [v_ref[0, blk * ppb + hp, hh] for hp in range(ppb)], axis=1).astype(BF16)

    def stage1():
        pl.when(i == 0)(prepare)
        kmean = kmean_scr[...]
        blk_iota = lax.broadcasted_iota(jnp.int32, (nblk, tq), 0)
        past = blk_iota < i
        causal = (lax.broadcasted_iota(jnp.int32, (MOBA_BLOCK, tq), 0)
                  <= lax.broadcasted_iota(jnp.int32, (MOBA_BLOCK, tq), 1))
        qbs = []
        for hh in range(heads):
            q_h = q_ref[0, :, hh * HEAD_DIM:(hh + 1) * HEAD_DIM]
            s_blk = _dot_nt_split(kmean[:, hh * HEAD_DIM:(hh + 1) * HEAD_DIM], q_h)
            s_blk = jnp.where(past, s_blk, -jnp.inf)
            rank = jnp.zeros((nblk, tq), jnp.int32)
            for mm in range(nblk):
                sm = s_blk[mm:mm + 1, :]
                rank = rank + jnp.where(sm > s_blk, 1, jnp.where((sm == s_blk) & (blk_iota > mm), 1, 0))
            bias_scr[hh] = jnp.where(past & (rank < MOBA_TOPK), 0.0, NEG)
            qbs.append((q_h * (HEAD_DIM ** -0.5)).astype(BF16))

        def scores(hh, j):
            kj = kb_scr[hh, pl.ds(pl.multiple_of(j * MOBA_BLOCK, MOBA_BLOCK), MOBA_BLOCK), :]
            return _dot_nt(kj, qbs[hh])

        mx_own = []
        for hh in range(heads):
            s_t = jnp.where(causal, scores(hh, i), NEG)
            s_scr[hh, i] = s_t
            mx_own.append(fold(s_t, jnp.max))

        def score_blocks(jj, mx):
            out = list(mx)
            for j in (2 * jj, jnp.minimum(2 * jj + 1, i - 1)):
                for hh in range(heads):
                    s_t = scores(hh, j) + bias_scr[hh, pl.ds(j, 1), :]
                    s_scr[hh, j] = s_t
                    out[hh] = jnp.maximum(out[hh], fold(s_t, jnp.max))
            return tuple(out)

        mx = lax.fori_loop(0, (i + 1) // 2, score_blocks, tuple(mx_own))
        for hh in range(heads):
            pmx_scr[hh] = mx[hh]

    def stage2():
        m_fin = [jnp.max(pmx_scr[hh], axis=0, keepdims=True) for hh in range(heads)]
        for hh in range(heads):
            acc_scr[hh] = jnp.zeros((HEAD_DIM, tq), F32)

        def apply_blocks(jj, ls):
            out = list(ls)
            for j, live in ((2 * jj, True), (jnp.minimum(2 * jj + 1, i), 2 * jj + 1 <= i)):
                for hh in range(heads):
                    shift = m_fin[hh] if live is True else jnp.where(live, m_fin[hh], jnp.inf)
                    p = jnp.exp(s_scr[hh, j] - shift)
                    out[hh] = out[hh] + fold(p, jnp.sum)
                    acc_scr[hh] = acc_scr[hh] + _dot(vt_scr[hh, j], p.astype(BF16))
            return tuple(out)

        ls = lax.fori_loop(0, i // 2 + 1, apply_blocks, tuple(jnp.zeros((8, tq), F32) for _ in range(heads)))
        o_ref[0] = jnp.concatenate(
            [acc_scr[hh] / jnp.sum(ls[hh], axis=0, keepdims=True) for hh in range(heads)], axis=0).T

    return stage1, stage2


def _prompt_scratch(s, tq, heads):
    nblk = s // MOBA_BLOCK
    return [pltpu.VMEM((heads, s, HEAD_DIM), BF16),
            pltpu.VMEM((heads, nblk, HEAD_DIM, MOBA_BLOCK), BF16),
            pltpu.VMEM((nblk, heads * HEAD_DIM), F32),
            pltpu.VMEM((heads, nblk, tq), F32),
            pltpu.VMEM((heads, nblk, MOBA_BLOCK, tq), F32),
            pltpu.VMEM((heads, HEAD_DIM, tq), F32),
            pltpu.VMEM((heads, 8, tq), F32)]


def _sample_stream(slot, pt_ref, q_ref, kn_ref, vn_ref, ck_hbm, cv_hbm, o_ref,
                   kbuf, vbuf, ksem, vsem, qbd_scr, s_scr, mx_scr, sb_scr, p_scr, pown_scr, l_scr, acc_scr,
                   *, n_seq, n_chunks, pages_per_step):
    ds = q_ref.shape[1]
    rows = N_HEADS * ds
    ppb = MOBA_BLOCK // PAGE_SIZE
    n_pages = n_chunks * pages_per_step
    n_blocks = n_pages // ppb
    n_global = (n_seq + 1) * n_chunks
    lookahead = SAMPLE_RING_SLOTS - 1
    lane = lax.broadcasted_iota(jnp.int32, (rows, 128), 1)
    row = lax.broadcasted_iota(jnp.int32, (rows, 128), 0)

    def page_copies(g, start):
        g_slot = g // n_chunks
        g_c = g % n_chunks
        ring = g % SAMPLE_RING_SLOTS

        def each(cache, buf, sem, seq):
            for p in range(pages_per_step):
                if start:
                    page = pt_ref[seq * n_pages + g_c * pages_per_step + p]
                    pltpu.make_async_copy(cache.at[page], buf.at[ring, p], sem.at[ring]).start()
                else:
                    pltpu.make_async_copy(cache.at[0], buf.at[ring, p], sem.at[ring]).wait()

        @pl.when(g_slot < n_seq)
        def _():
            each(ck_hbm, kbuf, ksem, g_slot)

        @pl.when(g_slot > 0)
        def _():
            each(cv_hbm, vbuf, vsem, g_slot - 1)

    def prime():
        for g in range(lookahead):
            page_copies(jnp.int32(g), start=True)

    def scaled_q():
        return (qbd_scr[...] * (HEAD_DIM ** -0.5)).astype(BF16)

    def finalize():
        sc = jnp.where(lane < n_blocks, sb_scr[...], -jnp.inf)
        lane_f = lane.astype(F32)
        sel = jnp.zeros((rows, 128), F32)
        for _ in range(min(MOBA_TOPK, n_blocks)):
            top = jnp.max(sc, axis=1, keepdims=True)
            first = jnp.min(jnp.where(sc == top, lane_f, 1e9), axis=1, keepdims=True)
            pick = lane_f == first
            sel = jnp.where(pick, 1.0, sel)
            sc = jnp.where(pick, -jnp.inf, sc)
        kn = jnp.concatenate([kn_ref[0], jnp.zeros((128 - ds, ATTN_WIDTH), F32)], axis=0)
        s_own = jnp.where(lane <= row % ds, _dot_nt(scaled_q(), kn.astype(BF16)), NEG)
        m_sel = jnp.max(jnp.where(sel > 0, mx_scr[...], NEG), axis=1, keepdims=True)
        m_fin = jnp.maximum(m_sel, jnp.max(s_own, axis=1, keepdims=True))
        p_own = jnp.exp(s_own - m_fin)
        pown_scr[...] = p_own
        p_sum = p_own
        for blk in range(n_blocks):
            chosen = sel[:, blk:blk + 1] > 0
            p = jnp.where(chosen, jnp.exp(s_scr[blk] - m_fin), 0.0)
            p_scr[blk] = p.astype(BF16)
            for hp in range(ppb):
                p_sum = p_sum + p[:, hp * PAGE_SIZE:(hp + 1) * PAGE_SIZE]
        l_scr[...] = jnp.broadcast_to(jnp.sum(p_sum, axis=1, keepdims=True), l_scr.shape)
        acc_scr[...] = jnp.zeros(acc_scr.shape, F32)

    def load_queries():
        q = q_ref[0]
        r_i = lax.broadcasted_iota(jnp.int32, (rows, ATTN_WIDTH), 0)
        l_i = lax.broadcasted_iota(jnp.int32, (rows, ATTN_WIDTH), 1)
        qbd_scr[...] = jnp.where(r_i // ds == l_i // HEAD_DIM, jnp.concatenate([q] * N_HEADS, axis=0), 0.0)
        mx_scr[...] = jnp.full(mx_scr.shape, NEG, F32)
        sb_scr[...] = jnp.zeros(sb_scr.shape, F32)

    def block_operand(buf, ring, bb):
        return jnp.concatenate(
            [buf[ring, bb * ppb + hp].reshape(ATTN_WIDTH, PAGE_SIZE).astype(BF16) for hp in range(ppb)], axis=1)

    def score_k_pages(c, ring):
        qb = scaled_q()
        mx = mx_scr[...]
        sb = sb_scr[...]
        for bb in range(pages_per_step // ppb):
            blk = c * (pages_per_step // ppb) + bb
            sc = _dot(qb, block_operand(kbuf, ring, bb))
            s_scr[blk] = sc
            mx = jnp.where(lane == blk, jnp.max(sc, axis=1, keepdims=True), mx)
            sb = jnp.where(lane == blk, jnp.sum(sc, axis=1, keepdims=True), sb)
        mx_scr[...] = mx
        sb_scr[...] = sb

    def apply_v_pages(c, ring):
        acc = acc_scr[...]
        for bb in range(pages_per_step // ppb):
            p = p_scr[c * (pages_per_step // ppb) + bb]
            acc = acc + _dot_nt(p, block_operand(vbuf, ring, bb))
        acc_scr[...] = acc

    def chunk(c, carry):
        g = slot * n_chunks + c
        ring = g % SAMPLE_RING_SLOTS

        @pl.when(g + lookahead < n_global)
        def _():
            page_copies(g + lookahead, start=True)

        page_copies(g, start=False)

        @pl.when((slot > 0) & (slot < n_seq))
        def _():
            score_k_pages(c, ring)
            apply_v_pages(c, ring)

        @pl.when(slot == 0)
        def _():
            score_k_pages(c, ring)

        @pl.when(slot == n_seq)
        def _():
            apply_v_pages(c, ring)

        return carry

    def write_output():
        vn = jnp.concatenate([vn_ref[0], jnp.zeros((128 - ds, ATTN_WIDTH), F32)], axis=0)
        full = acc_scr[...] + _dot(pown_scr[...].astype(BF16), vn.astype(BF16))
        full = full / l_scr[:, 0:1]
        l_i = lax.broadcasted_iota(jnp.int32, (ds, ATTN_WIDTH), 1)
        out = jnp.zeros((ds, ATTN_WIDTH), F32)
        for hh in range(N_HEADS):
            out = out + jnp.where(l_i // HEAD_DIM == hh, full[hh * ds:(hh + 1) * ds, :], 0.0)
        o_ref[0] = out

    def begin():
        pl.when(slot == 0)(prime)
        pl.when(slot > 0)(finalize)
        pl.when(slot < n_seq)(load_queries)

    def run_chunks(c0, c1):
        lax.fori_loop(c0, c1, chunk, 0)

    def end():
        pl.when(slot > 0)(write_output)

    return begin, run_chunks, end


def _sample_scratch(ds, n_pages, pps):
    rows = N_HEADS * ds
    n_blocks = n_pages * PAGE_SIZE // MOBA_BLOCK
    ring_buf = pltpu.VMEM((SAMPLE_RING_SLOTS, pps, N_HEADS, HEAD_DIM, PAGE_SIZE), F32)
    return [ring_buf, ring_buf,
            pltpu.SemaphoreType.DMA((SAMPLE_RING_SLOTS,)),
            pltpu.SemaphoreType.DMA((SAMPLE_RING_SLOTS,)),
            pltpu.VMEM((rows, ATTN_WIDTH), F32),
            pltpu.VMEM((n_blocks, rows, MOBA_BLOCK), F32),
            pltpu.VMEM((rows, 128), F32),
            pltpu.VMEM((rows, 128), F32),
            pltpu.VMEM((n_blocks, rows, MOBA_BLOCK), BF16),
            pltpu.VMEM((rows, 128), F32),
            pltpu.VMEM((rows, 128), F32),
            pltpu.VMEM((rows, ATTN_WIDTH), F32)]


def _moba_body(pt_ref, qs_ref, kn_ref, vn_ref, ck_hbm, cv_hbm, qp_ref, kp_ref, vp_ref, os_ref, op_ref, *scratch,
               n_seq, n_chunks, pages_per_step, n_prompt_steps, n_qtiles, n_sample_scratch):
    t = pl.program_id(0)
    begin, run_chunks, end = _sample_stream(
        t, pt_ref, qs_ref, kn_ref, vn_ref, ck_hbm, cv_hbm, os_ref, *scratch[:n_sample_scratch],
        n_seq=n_seq, n_chunks=n_chunks, pages_per_step=pages_per_step)
    stage1, stage2 = _prompt_attention_stages(t % n_qtiles, qp_ref, kp_ref, vp_ref, op_ref,
                                              *scratch[n_sample_scratch:])
    n_steps = max(n_seq + 1, n_prompt_steps)
    sample_part = (lambda f: f()) if n_steps == n_seq + 1 else (lambda f: pl.when(t <= n_seq)(f))
    prompt_part = (lambda f: f()) if n_steps == n_prompt_steps else (lambda f: pl.when(t < n_prompt_steps)(f))
    c1, c2 = n_chunks // 4, (5 * n_chunks) // 8

    def sample_head():
        begin()
        run_chunks(0, c1)

    def sample_middle():
        run_chunks(c1, c2)

    def sample_tail():
        run_chunks(c2, n_chunks)
        end()

    sample_part(sample_head)
    prompt_part(stage1)
    sample_part(sample_middle)
    prompt_part(stage2)
    sample_part(sample_tail)


def _moba(q_p, k_paged, v_paged, q_s, k_new, v_new, cache_k, cache_v, page_table):
    b, s, w = q_p.shape
    db, ds, _ = q_s.shape
    n_pages = page_table.shape[1]
    assert (n_pages * PAGE_SIZE) % MOBA_BLOCK == 0, "past length must be whole MoBA blocks"
    pps = SAMPLE_PAGES_PER_STEP
    n_chunks = n_pages // pps
    assert n_chunks * pps == n_pages and pps % (MOBA_BLOCK // PAGE_SIZE) == 0
    ck = jnp.transpose(cache_k, (0, 2, 3, 1))
    cv = jnp.transpose(cache_v, (0, 2, 3, 1))
    pt = page_table.reshape(-1)

    tq = MOBA_BLOCK
    heads = PROMPT_HEADS_PER_STEP
    lanes = heads * HEAD_DIM
    n_qt, n_hg = s // tq, w // lanes
    n_prompt = b * n_hg * n_qt
    n_steps = max(db + 1, n_prompt)

    def prompt_index(t):
        tp = jnp.minimum(t, n_prompt - 1)
        return tp // (n_hg * n_qt), (tp // n_qt) % n_hg, tp % n_qt

    def q_map(t, pt_ref):
        bi, hg, i = prompt_index(t)
        return (bi, i, hg)

    def kv_map(t, pt_ref):
        bi, hg, _ = prompt_index(t)
        return (bi, 0, hg, 0, 0)

    cur = pl.BlockSpec((1, ds, w), lambda t, pt_ref: (jnp.minimum(t, db - 1), 0, 0))
    prev = pl.BlockSpec((1, ds, w), lambda t, pt_ref: (jnp.clip(t - 1, 0, db - 1), 0, 0))
    hbm = pl.BlockSpec(memory_space=pl.ANY)
    q_spec = pl.BlockSpec((1, tq, lanes), q_map)
    kv_spec = pl.BlockSpec((1, s // PAGE_SIZE, heads, HEAD_DIM, PAGE_SIZE), kv_map)
    sample_scratch = _sample_scratch(ds, n_pages, pps)
    grid_spec = pltpu.PrefetchScalarGridSpec(
        num_scalar_prefetch=1,
        grid=(n_steps,),
        in_specs=[cur, prev, prev, hbm, hbm, q_spec, kv_spec, kv_spec],
        out_specs=[prev, q_spec],
        scratch_shapes=sample_scratch + _prompt_scratch(s, tq, heads))
    body = functools.partial(_moba_body, n_seq=db, n_chunks=n_chunks, pages_per_step=pps,
                             n_prompt_steps=n_prompt, n_qtiles=n_qt, n_sample_scratch=len(sample_scratch))
    y_s, y_p = pl.pallas_call(
        body,
        grid_spec=grid_spec,
        out_shape=[jax.ShapeDtypeStruct((db, ds, w), F32), jax.ShapeDtypeStruct((b, s, w), F32)],
        compiler_params=pltpu.CompilerParams(dimension_semantics=("arbitrary",),
                                             vmem_limit_bytes=VMEM_LIMIT),
        name="moba",
    )(pt, q_s, k_new, v_new, ck, cv, q_p, k_paged, v_paged)
    return y_p, y_s


def _mixer_out_body(x_ref, yc_ref, ya_ref, ag_ref, gmc_ref, gma_ref, gate_ref, wb_ref, wo_ref, fg_ref, o_ref):
    gb, tb, d = x_ref.shape
    m = gb * tb
    yc = yc_ref[...].reshape(m, CONV_CH).astype(BF16)
    ya = (ya_ref[...] * _silu(ag_ref[...])).reshape(m, ATTN_WIDTH).astype(BF16)
    merged = (jax.nn.sigmoid(gmc_ref[...].reshape(m, d)) * _dot(yc, wb_ref[0])
              + jax.nn.sigmoid(gma_ref[...].reshape(m, d)) * _dot(ya, wb_ref[1]))
    o = _dot(merged.astype(BF16), wo_ref[...]).reshape(gb, tb, d)
    xo = x_ref[...] + gate_ref[...] * o
    r = xo * lax.rsqrt(jnp.mean(xo * xo, axis=-1, keepdims=True) + EPS)
    o_ref[...] = r * fg_ref[...]


def _mixer_out(x3, y_conv, y_attn, ag, gmc, gma, gate, wb_bf, wo_bf, final_g, gb, tb):
    g, t, d = x3.shape
    xmap = lambda i, j: (i, j, 0)
    bmap = lambda i, j: (i, 0, 0)
    spec = lambda w: pl.BlockSpec((gb, tb, w), xmap)
    return pl.pallas_call(
        _mixer_out_body,
        grid=(g // gb, t // tb),
        in_specs=[spec(d), spec(CONV_CH), spec(ATTN_WIDTH), spec(ATTN_WIDTH), spec(d), spec(d),
                  pl.BlockSpec((gb, 1, d), bmap),
                  pl.BlockSpec((2, CONV_CH, d), lambda i, j: (0, 0, 0)),
                  pl.BlockSpec((d, d), lambda i, j: (0, 0)),
                  pl.BlockSpec((1, d), lambda i, j: (0, 0))],
        out_specs=spec(d),
        out_shape=jax.ShapeDtypeStruct((g, t, d), F32),
        compiler_params=pltpu.CompilerParams(dimension_semantics=("parallel", "parallel"),
                                             vmem_limit_bytes=VMEM_LIMIT),
        name="mixer_out",
    )(x3, y_conv, y_attn, ag, gmc, gma, gate, wb_bf, wo_bf, final_g)


def kernel(x_prompt, x_sample, cache_k, cache_v, state_conv, page_table, c_prompt, c_sample, norm_g, w_ada,
           b_ada, w_in, conv_w, conv_b, conv_ln_g, conv_ln_b, w_branch, w_out, final_g):
    depth = norm_g.shape[0]
    assert depth == 1, "single-layer trunk"
    b, s, d = x_prompt.shape
    db, ds, _ = x_sample.shape
    assert CONV_CH == ATTN_WIDTH and s % MOBA_BLOCK == 0 and s >= CONV_WIDTH - 1
    past_len = page_table.shape[1] * PAGE_SIZE
    lyr = 0

    ada = _ada(jnp.concatenate([c_prompt, c_sample], axis=0), w_ada[lyr], b_ada[lyr])
    shift, scale, gate = (ada[:, j * d:(j + 1) * d][:, None, :] for j in range(3))
    w_in_bf = w_in[lyr].astype(BF16)
    wb_bf = w_branch[lyr].astype(BF16)
    wo_bf = w_out[lyr].astype(BF16)
    g_in = norm_g[lyr].reshape(1, d)
    g_fin = final_g.reshape(1, d)
    cw, cb = conv_w[lyr], conv_b[lyr].reshape(1, CONV_CH)
    lg, lb = conv_ln_g[lyr].reshape(1, CONV_CH), conv_ln_b[lyr].reshape(1, CONV_CH)

    tab_p = _rope_tables(jnp.arange(s))
    yc_p, glu_tail, q_p, k_p, v_p, ag_p, gmc_p, gma_p = _mixer_in(
        x_prompt, scale[:b], shift[:b], g_in, w_in_bf, tab_p, 1, ROW_TILE, conv_params=(cw, cb, lg, lb))
    gbs = ROW_TILE // ds
    tab_s = tuple(jnp.tile(a, (gbs, 1)) for a in _rope_tables(past_len + jnp.arange(ds)))
    glu_s, cg_s, q_s, k_s, v_s, ag_s, gmc_s, gma_s = _mixer_in(
        x_sample, scale[b:], shift[b:], g_in, w_in_bf, tab_s, gbs, ds)
    hist = jnp.concatenate([state_conv[lyr], glu_s], axis=1)
    yc_s = _conv_sample(hist, cg_s, cw, cb, lg, lb)

    ya_p, ya_s = _moba(q_p, k_p, v_p, q_s, k_s, v_s, cache_k[lyr], cache_v[lyr], page_table)

    y_prompt = _mixer_out(x_prompt, yc_p, ya_p, ag_p, gmc_p, gma_p, gate[:b], wb_bf, wo_bf, g_fin, 1, OUT_ROW_TILE)
    y_sample = _mixer_out(x_sample, yc_s, ya_s, ag_s, gmc_s, gma_s, gate[b:], wb_bf, wo_bf, g_fin,
                          OUT_ROW_TILE // ds, ds)
    k_prompt = jnp.transpose(k_p, (0, 1, 4, 2, 3))[None]
    v_prompt = jnp.transpose(v_p, (0, 1, 4, 2, 3))[None]
    conv_prompt = glu_tail[None, :, HIST_PAD - (CONV_WIDTH - 1):, :]
    k_sample = k_s.reshape(1, db, ds, N_HEADS, HEAD_DIM)
    v_sample = v_s.reshape(1, db, ds, N_HEADS, HEAD_DIM)
    conv_sample = hist[None, :, ds:, :]

    return (y_prompt, y_sample, k_prompt, v_prompt, conv_prompt, k_sample, v_sample, conv_sample)
```

```python
import functools

import jax
import jax.numpy as jnp
from jax import lax
from jax.experimental import pallas as pl
from jax.experimental.pallas import tpu as pltpu

F32 = jnp.float32
BF16 = jnp.bfloat16

D_MODEL = 1024
HEAD_DIM = 64
N_HEADS = 8
ATTN_WIDTH = N_HEADS * HEAD_DIM
CONV_CH = 512
ROT_DIM = HEAD_DIM // 4
ROPE_THETA = 500000.0
MOBA_BLOCK = 256
MOBA_TOPK = 3
CONV_WIDTH = 31
PAGE_SIZE = 128
EPS = 1e-6
IN_SIZES = (CONV_CH, CONV_CH, CONV_CH, ATTN_WIDTH, ATTN_WIDTH, ATTN_WIDTH, ATTN_WIDTH, D_MODEL, D_MODEL)
IN_COLS = sum(IN_SIZES)
NEG = -1e30
SCORE_SCALE = HEAD_DIM ** -0.5 * 1.4426950408889634

ROPE_TABLE_LANES = 128
ROW_TILE = 256
OUT_ROW_TILE = 512
CONV_SUB = 32
HIST_PAD = 32
PROMPT_HEADS_PER_STEP = 4
SAMPLE_PAGES_PER_STEP = 8
SAMPLE_RING_SLOTS = 5
VMEM_LIMIT = 56 * 1024 * 1024


def _silu(x):
    return x * jax.nn.sigmoid(x)


def _dot(a, b):
    return jnp.dot(a, b, preferred_element_type=F32)


def _dot_nt(a, b):
    return lax.dot_general(a, b, (((1,), (1,)), ((), ())), preferred_element_type=F32)


def _dot_nt_split(a, b):
    a_hi, b_hi = a.astype(BF16), b.astype(BF16)
    a_lo = (a - a_hi.astype(F32)).astype(BF16)
    b_lo = (b - b_hi.astype(F32)).astype(BF16)
    return _dot_nt(a_hi, b_hi) + (_dot_nt(a_hi, b_lo) + _dot_nt(a_lo, b_hi))


def _ada_body(c_ref, w_ref, b_ref, o_ref):
    a = _silu(c_ref[...]).astype(BF16)
    o_ref[...] = _dot(a, w_ref[...].astype(BF16)) + b_ref[...]


def _ada(c_all, w_ada, b_ada):
    n = c_all.shape[0]
    return pl.pallas_call(
        _ada_body,
        grid=(3,),
        in_specs=[pl.BlockSpec((n, D_MODEL), lambda j: (0, 0)),
                  pl.BlockSpec((D_MODEL, D_MODEL), lambda j: (0, j)),
                  pl.BlockSpec((1, D_MODEL), lambda j: (0, j))],
        out_specs=pl.BlockSpec((n, D_MODEL), lambda j: (0, j)),
        out_shape=jax.ShapeDtypeStruct((n, 3 * D_MODEL), F32),
        compiler_params=pltpu.CompilerParams(dimension_semantics=("arbitrary",),
                                             vmem_limit_bytes=VMEM_LIMIT),
        name="ada",
    )(c_all, w_ada, b_ada.reshape(1, 3 * D_MODEL))


def _rope_rows(x, cos, sin_lo, sin_hi):
    n = x.shape[-1]
    half = ROT_DIM // 2
    return x * cos + pltpu.roll(x, n - half, 1) * sin_lo + pltpu.roll(x, half, 1) * sin_hi


def _mixer_in_body(*refs, prompt):
    x_ref, scale_ref, shift_ref, g_ref, w_ref, cos_ref, slo_ref, shi_ref = refs[:8]
    if prompt:
        cw_ref, cb_ref, lg_ref, lb_ref = refs[8:12]
        yconv_ref, tail_ref, q_ref, k_ref, v_ref, ag_ref, gmc_ref, gma_ref, hist_scr, shift_scr = refs[12:]
    else:
        glu_ref, cg_ref, q_ref, k_ref, v_ref, ag_ref, gmc_ref, gma_ref = refs[8:]
    gb, tb, d = x_ref.shape
    m = gb * tb
    if prompt:
        _carry_conv_history(pl.program_id(1) == 0, hist_scr, tb)
    x = x_ref[...]
    r = x * lax.rsqrt(jnp.mean(x * x, axis=-1, keepdims=True) + EPS)
    h = (r * g_ref[...]) * (1.0 + scale_ref[...]) + shift_ref[...]
    hb = h.reshape(m, d).astype(BF16)
    starts = [sum(IN_SIZES[:idx]) for idx in range(len(IN_SIZES))]
    half = D_MODEL // 2

    def cols(lo, width):
        return _dot(hb, w_ref[:, lo:lo + width])

    def put(ref, val):
        ref[...] = val.reshape(ref.shape)

    def put_half(ref, lo, part):
        ref[:, :, part * half:(part + 1) * half] = cols(lo + part * half, half).reshape(gb, tb, half)

    def put_kv(ref, val):
        if not prompt:
            return put(ref, val)
        val_t = val.T
        for pg in range(m // PAGE_SIZE):
            ref[0, pg] = val_t[:, pg * PAGE_SIZE:(pg + 1) * PAGE_SIZE].reshape(N_HEADS, HEAD_DIM, PAGE_SIZE)

    def rope(val):
        wide = lambda ref: jnp.concatenate([ref[...]] * (ATTN_WIDTH // ROPE_TABLE_LANES), axis=1)
        return _rope_rows(val, wide(cos_ref), wide(slo_ref), wide(shi_ref))

    glu = cols(starts[0], CONV_CH) * jax.nn.sigmoid(cols(starts[1], CONV_CH))
    cg = cols(starts[2], CONV_CH)
    pieces = [lambda: put(q_ref, rope(cols(starts[3], ATTN_WIDTH))),
              lambda: put_kv(k_ref, rope(cols(starts[4], ATTN_WIDTH))),
              lambda: put_kv(v_ref, cols(starts[5], ATTN_WIDTH)),
              lambda: put(ag_ref, cols(starts[6], ATTN_WIDTH)),
              lambda: put_half(gmc_ref, starts[7], 0), lambda: put_half(gmc_ref, starts[7], 1),
              lambda: put_half(gma_ref, starts[8], 0), lambda: put_half(gma_ref, starts[8], 1)]
    if prompt:
        conv_chunks = _causal_conv_tile(glu, cg, cw_ref, cb_ref, lg_ref, lb_ref, yconv_ref, tail_ref,
                                        hist_scr, shift_scr)
    else:
        put(glu_ref, glu)
        put(cg_ref, cg)
        conv_chunks = []
    per_piece = -(-len(conv_chunks) // len(pieces))
    for idx, piece in enumerate(pieces):
        piece()
        for chunk in conv_chunks[idx * per_piece:(idx + 1) * per_piece]:
            chunk()


def _mixer_in(x3, scale, shift, norm_g, w_in_bf, tables, gb, tb, conv_params=None):
    g, t, d = x3.shape
    m = gb * tb
    prompt = conv_params is not None
    grid = (g // gb, t // tb)
    xmap = lambda i, j: (i, j, 0)
    bmap = lambda i, j: (i, 0, 0)
    cmap = lambda i, j: (0, 0)
    tmap = lambda i, j: (j, 0)
    widths = (CONV_CH, CONV_CH, ATTN_WIDTH, ATTN_WIDTH, ATTN_WIDTH, ATTN_WIDTH, D_MODEL, D_MODEL)
    out_specs = [pl.BlockSpec((gb, tb, w), xmap) for w in widths]
    out_shape = [jax.ShapeDtypeStruct((g, t, w), F32) for w in widths]
    in_specs = [pl.BlockSpec((gb, tb, d), xmap),
                pl.BlockSpec((gb, 1, d), bmap),
                pl.BlockSpec((gb, 1, d), bmap),
                pl.BlockSpec((1, d), cmap),
                pl.BlockSpec((d, IN_COLS), cmap, pipeline_mode=pl.Buffered(1)),
                pl.BlockSpec((m, ROPE_TABLE_LANES), tmap),
                pl.BlockSpec((m, ROPE_TABLE_LANES), tmap),
                pl.BlockSpec((m, ROPE_TABLE_LANES), tmap)]
    operands = [x3, scale, shift, norm_g, w_in_bf, *tables]
    scratch = []
    if prompt:
        assert gb == 1 and tb % PAGE_SIZE == 0 and tb >= HIST_PAD
        out_shape[0] = jax.ShapeDtypeStruct((g, t, CONV_CH), BF16)
        out_specs[1] = pl.BlockSpec((1, HIST_PAD, CONV_CH), bmap)
        out_shape[1] = jax.ShapeDtypeStruct((g, HIST_PAD, CONV_CH), F32)
        for idx in (3, 4):
            out_specs[idx] = pl.BlockSpec((1, tb // PAGE_SIZE, N_HEADS, HEAD_DIM, PAGE_SIZE),
                                          lambda i, j: (i, j, 0, 0, 0))
            out_shape[idx] = jax.ShapeDtypeStruct((g, t // PAGE_SIZE, N_HEADS, HEAD_DIM, PAGE_SIZE), F32)
        in_specs += [pl.BlockSpec((CONV_WIDTH, CONV_CH), cmap)] + [pl.BlockSpec((1, CONV_CH), cmap)] * 3
        operands += list(conv_params)
        scratch = [pltpu.VMEM((HIST_PAD + tb, CONV_CH), F32),
                   pltpu.VMEM((7, _conv_shift_span(tb), CONV_CH), F32)]
    return pl.pallas_call(
        functools.partial(_mixer_in_body, prompt=prompt),
        grid=grid,
        in_specs=in_specs,
        out_specs=out_specs,
        out_shape=out_shape,
        scratch_shapes=scratch,
        compiler_params=pltpu.CompilerParams(
            dimension_semantics=("parallel", "arbitrary" if prompt else "parallel"),
            vmem_limit_bytes=VMEM_LIMIT),
        name="mixer_in",
    )(*operands)


def _rope_tables(pos):
    inv = ROPE_THETA ** (-(jnp.arange(0, ROT_DIM, 2, dtype=F32) / ROT_DIM))
    ang = pos.astype(F32)[:, None] * inv[None, :]
    cos, sin = jnp.cos(ang), jnp.sin(ang)
    half = ROT_DIM // 2
    n = pos.shape[0]
    pad = jnp.zeros((n, HEAD_DIM - ROT_DIM), F32)
    zero = jnp.zeros((n, half), F32)
    cos_h = jnp.concatenate([cos, cos, pad + 1.0], axis=1)
    slo_h = jnp.concatenate([-sin, zero, pad], axis=1)
    shi_h = jnp.concatenate([zero, sin, pad], axis=1)
    return tuple(jnp.tile(a, (1, ROPE_TABLE_LANES // HEAD_DIM)) for a in (cos_h, slo_h, shi_h))


def _conv_epilogue(y, cg, b, lg, lb):
    y = y + b
    mu = jnp.mean(y, axis=-1, keepdims=True)
    yc = y - mu
    var = jnp.mean(yc * yc, axis=-1, keepdims=True)
    z = _silu(yc * lax.rsqrt(var + EPS) * lg + lb)
    return z * _silu(cg)


def _conv_tap_offsets():
    off = HIST_PAD - (CONV_WIDTH - 1)
    return [divmod(off + tap, 8) for tap in range(CONV_WIDTH)]


def _conv_shift_span(tc):
    return tc + 8 * max(a for a, r in _conv_tap_offsets() if r)


def _carry_conv_history(first, hist_scr, tc):
    @pl.when(first)
    def _():
        hist_scr[0:HIST_PAD, :] = jnp.zeros((HIST_PAD, CONV_CH), F32)

    @pl.when(jnp.logical_not(first))
    def _():
        hist_scr[0:HIST_PAD, :] = hist_scr[tc:tc + HIST_PAD, :]


def _causal_conv_tile(glu, cg, w_ref, b_ref, lg_ref, lb_ref, o_ref, tail_ref, hist_scr, shift_scr):
    tc = glu.shape[0]
    hist_scr[HIST_PAD:HIST_PAD + tc, :] = glu
    tail_ref[0] = glu[tc - HIST_PAD:, :]
    span = _conv_shift_span(tc)
    for r in range(1, 8):
        shift_scr[r - 1] = hist_scr[r:r + span, :]

    def rows(r0):
        acc = jnp.zeros((CONV_SUB, CONV_CH), F32)
        for tap, (a, r) in enumerate(_conv_tap_offsets()):
            lo = r0 + 8 * a
            win = hist_scr[lo:lo + CONV_SUB, :] if r == 0 else shift_scr[r - 1, lo:lo + CONV_SUB, :]
            acc = acc + w_ref[tap:tap + 1, :] * win
        o_ref[0, r0:r0 + CONV_SUB, :] = _conv_epilogue(
            acc, cg[r0:r0 + CONV_SUB, :], b_ref[...], lg_ref[...], lb_ref[...]).astype(o_ref.dtype)

    return [functools.partial(rows, c * CONV_SUB) for c in range(tc // CONV_SUB)]


def _conv_sample_body(hist_ref, cg_ref, w_ref, b_ref, lg_ref, lb_ref, o_ref):
    nb, t, c = cg_ref.shape
    acc = jnp.zeros((nb, t, c), F32)
    for tap in range(CONV_WIDTH):
        acc = acc + w_ref[tap:tap + 1, :] * hist_ref[:, tap:tap + t, :]
    o_ref[...] = _conv_epilogue(acc, cg_ref[...], b_ref[...], lg_ref[...], lb_ref[...])


def _conv_sample(hist, cg, conv_w, conv_b, ln_g, ln_b):
    n, t, c = cg.shape
    nb = 8
    cmap = lambda i: (0, 0)
    return pl.pallas_call(
        _conv_sample_body,
        grid=(n // nb,),
        in_specs=[pl.BlockSpec((nb, hist.shape[1], c), lambda i: (i, 0, 0)),
                  pl.BlockSpec((nb, t, c), lambda i: (i, 0, 0)),
                  pl.BlockSpec((CONV_WIDTH, c), cmap),
                  pl.BlockSpec((1, c), cmap),
                  pl.BlockSpec((1, c), cmap),
                  pl.BlockSpec((1, c), cmap)],
        out_specs=pl.BlockSpec((nb, t, c), lambda i: (i, 0, 0)),
        out_shape=jax.ShapeDtypeStruct((n, t, c), F32),
        compiler_params=pltpu.CompilerParams(dimension_semantics=("parallel",),
                                             vmem_limit_bytes=VMEM_LIMIT),
        name="conv_sample",
    )(hist, cg, conv_w, conv_b, ln_g, ln_b)


def _prompt_attention_stages(i, q_ref, k_ref, v_ref, o_ref, kb_scr, vt_scr, kmean_scr, bias_scr, s_scr, acc_scr,
                             pmx_scr):
    tq = q_ref.shape[1]
    n_pages, heads = k_ref.shape[1], k_ref.shape[2]
    ppb = MOBA_BLOCK // PAGE_SIZE
    nblk = n_pages // ppb

    def fold(x, op):
        return op(x.reshape(MOBA_BLOCK // 8, 8, tq), axis=0)

    def prepare():
        for blk in range(nblk):
            ksum = jnp.zeros((1, heads * HEAD_DIM), F32)
            for hp in range(ppb):
                pg = blk * ppb + hp
                k_rows = k_ref[0, pg].reshape(heads * HEAD_DIM, PAGE_SIZE).T
                ksum = ksum + jnp.sum(k_rows, axis=0, keepdims=True)
                for hh in range(heads):
                    kb_scr[hh, pg * PAGE_SIZE:(pg + 1) * PAGE_SIZE, :] = (
                        k_rows[:, hh * HEAD_DIM:(hh + 1) * HEAD_DIM].astype(BF16))
            kmean_scr[blk:blk + 1, :] = ksum * (1.0 / MOBA_BLOCK)
            for hh in range(heads):
                vt_scr[hh, blk] = jnp.concatenate(
                    [v_ref[0, blk * ppb + hp, hh] for hp in range(ppb)], axis=1).astype(BF16)

    def stage1():
        pl.when(i == 0)(prepare)
        kmean = kmean_scr[...]
        blk_iota = lax.broadcasted_iota(jnp.int32, (nblk, tq), 0)
        past = blk_iota < i
        causal = (lax.broadcasted_iota(jnp.int32, (MOBA_BLOCK, tq), 0)
                  <= lax.broadcasted_iota(jnp.int32, (MOBA_BLOCK, tq), 1))
        qbs = []
        for hh in range(heads):
            q_h = q_ref[0, :, hh * HEAD_DIM:(hh + 1) * HEAD_DIM]
            s_blk = _dot_nt_split(kmean[:, hh * HEAD_DIM:(hh + 1) * HEAD_DIM], q_h)
            s_blk = jnp.where(past, s_blk, -jnp.inf)
            rank = jnp.zeros((nblk, tq), jnp.int32)
            for mm in range(nblk):
                sm = s_blk[mm:mm + 1, :]
                rank = rank + jnp.where(sm > s_blk, 1, jnp.where((sm == s_blk) & (blk_iota > mm), 1, 0))
            bias_scr[hh] = jnp.where(past & (rank < MOBA_TOPK), 0.0, NEG)
            qbs.append((q_h * SCORE_SCALE).astype(BF16))

        def scores(hh, j):
            kj = kb_scr[hh, pl.ds(pl.multiple_of(j * MOBA_BLOCK, MOBA_BLOCK), MOBA_BLOCK), :]
            return _dot_nt(kj, qbs[hh])

        mx_own = []
        for hh in range(heads):
            s_t = jnp.where(causal, scores(hh, i), NEG)
            s_scr[hh, i] = s_t
            mx_own.append(fold(s_t, jnp.max))

        def score_blocks(jj, mx):
            out = list(mx)
            for j in (2 * jj, jnp.minimum(2 * jj + 1, i - 1)):
                for hh in range(heads):
                    s_t = scores(hh, j) + bias_scr[hh, pl.ds(j, 1), :]
                    s_scr[hh, j] = s_t
                    out[hh] = jnp.maximum(out[hh], fold(s_t, jnp.max))
            return tuple(out)

        mx = lax.fori_loop(0, (i + 1) // 2, score_blocks, tuple(mx_own))
        for hh in range(heads):
            pmx_scr[hh] = mx[hh]

    def stage2():
        m_fin = [jnp.max(pmx_scr[hh], axis=0, keepdims=True) for hh in range(heads)]
        for hh in range(heads):
            acc_scr[hh] = jnp.zeros((HEAD_DIM, tq), F32)

        def apply_blocks(jj, ls):
            out = list(ls)
            for j, live in ((2 * jj, True), (jnp.minimum(2 * jj + 1, i), 2 * jj + 1 <= i)):
                for hh in range(heads):
                    shift = m_fin[hh] if live is True else jnp.where(live, m_fin[hh], jnp.inf)
                    p = jnp.exp2(s_scr[hh, j] - shift)
                    out[hh] = out[hh] + fold(p, jnp.sum)
                    acc_scr[hh] = acc_scr[hh] + _dot(vt_scr[hh, j], p.astype(BF16))
            return tuple(out)

        ls = lax.fori_loop(0, i // 2 + 1, apply_blocks, tuple(jnp.zeros((8, tq), F32) for _ in range(heads)))
        o_ref[0] = jnp.concatenate(
            [acc_scr[hh] / jnp.sum(ls[hh], axis=0, keepdims=True) for hh in range(heads)], axis=0).T

    return stage1, stage2


def _prompt_scratch(s, tq, heads):
    nblk = s // MOBA_BLOCK
    return [pltpu.VMEM((heads, s, HEAD_DIM), BF16),
            pltpu.VMEM((heads, nblk, HEAD_DIM, MOBA_BLOCK), BF16),
            pltpu.VMEM((nblk, heads * HEAD_DIM), F32),
            pltpu.VMEM((heads, nblk, tq), F32),
            pltpu.VMEM((heads, nblk, MOBA_BLOCK, tq), F32),
            pltpu.VMEM((heads, HEAD_DIM, tq), F32),
            pltpu.VMEM((heads, 8, tq), F32)]


def _sample_stream(slot, pt_ref, q_ref, kn_ref, vn_ref, ck_hbm, cv_hbm, o_ref,
                   kbuf, vbuf, ksem, vsem, qbd_scr, s_scr, mx_scr, sb_scr, p_scr, pown_scr, l_scr, acc_scr,
                   *, n_seq, n_chunks, pages_per_step):
    ds = q_ref.shape[1]
    rows = N_HEADS * ds
    ppb = MOBA_BLOCK // PAGE_SIZE
    n_pages = n_chunks * pages_per_step
    n_blocks = n_pages // ppb
    n_global = (n_seq + 1) * n_chunks
    lookahead = SAMPLE_RING_SLOTS - 1
    lane = lax.broadcasted_iota(jnp.int32, (rows, 128), 1)
    row = lax.broadcasted_iota(jnp.int32, (rows, 128), 0)

    def page_copies(g, start):
        g_slot = g // n_chunks
        g_c = g % n_chunks
        ring = g % SAMPLE_RING_SLOTS

        def each(cache, buf, sem, seq):
            for p in range(pages_per_step):
                if start:
                    page = pt_ref[seq * n_pages + g_c * pages_per_step + p]
                    pltpu.make_async_copy(cache.at[page], buf.at[ring, p], sem.at[ring]).start()
                else:
                    pltpu.make_async_copy(cache.at[0], buf.at[ring, p], sem.at[ring]).wait()

        @pl.when(g_slot < n_seq)
        def _():
            each(ck_hbm, kbuf, ksem, g_slot)

        @pl.when(g_slot > 0)
        def _():
            each(cv_hbm, vbuf, vsem, g_slot - 1)

    def prime():
        for g in range(lookahead):
            page_copies(jnp.int32(g), start=True)

    def scaled_q():
        return (qbd_scr[...] * SCORE_SCALE).astype(BF16)

    def finalize():
        sc = jnp.where(lane < n_blocks, sb_scr[...], -jnp.inf)
        lane_f = lane.astype(F32)
        sel = jnp.zeros((rows, 128), F32)
        for _ in range(min(MOBA_TOPK, n_blocks)):
            top = jnp.max(sc, axis=1, keepdims=True)
            first = jnp.min(jnp.where(sc == top, lane_f, 1e9), axis=1, keepdims=True)
            pick = lane_f == first
            sel = jnp.where(pick, 1.0, sel)
            sc = jnp.where(pick, -jnp.inf, sc)
        kn = jnp.concatenate([kn_ref[0], jnp.zeros((128 - ds, ATTN_WIDTH), F32)], axis=0)
        s_own = jnp.where(lane <= row % ds, _dot_nt(scaled_q(), kn.astype(BF16)), NEG)
        m_sel = jnp.max(jnp.where(sel > 0, mx_scr[...], NEG), axis=1, keepdims=True)
        m_fin = jnp.maximum(m_sel, jnp.max(s_own, axis=1, keepdims=True))
        p_own = jnp.exp2(s_own - m_fin)
        pown_scr[...] = p_own
        p_sum = p_own
        for blk in range(n_blocks):
            shift = jnp.where(sel[:, blk:blk + 1] > 0, m_fin, jnp.inf)
            p = jnp.exp2(s_scr[blk] - shift)
            p_scr[blk] = p.astype(BF16)
            for hp in range(ppb):
                p_sum = p_sum + p[:, hp * PAGE_SIZE:(hp + 1) * PAGE_SIZE]
        l_scr[...] = jnp.broadcast_to(jnp.sum(p_sum, axis=1, keepdims=True), l_scr.shape)
        acc_scr[...] = jnp.zeros(acc_scr.shape, F32)

    def load_queries():
        q = q_ref[0]
        r_i = lax.broadcasted_iota(jnp.int32, (rows, ATTN_WIDTH), 0)
        l_i = lax.broadcasted_iota(jnp.int32, (rows, ATTN_WIDTH), 1)
        qbd_scr[...] = jnp.where(r_i // ds == l_i // HEAD_DIM, jnp.concatenate([q] * N_HEADS, axis=0), 0.0)
        mx_scr[...] = jnp.full(mx_scr.shape, NEG, F32)
        sb_scr[...] = jnp.zeros(sb_scr.shape, F32)

    def block_operand(buf, ring, bb):
        return jnp.concatenate(
            [buf[ring, bb * ppb + hp].reshape(ATTN_WIDTH, PAGE_SIZE).astype(BF16) for hp in range(ppb)], axis=1)

    def score_k_pages(c, ring):
        qb = scaled_q()
        mx = mx_scr[...]
        sb = sb_scr[...]
        for bb in range(pages_per_step // ppb):
            blk = c * (pages_per_step // ppb) + bb
            sc = _dot(qb, block_operand(kbuf, ring, bb))
            s_scr[blk] = sc
            mx = jnp.where(lane == blk, jnp.max(sc, axis=1, keepdims=True), mx)
            sb = jnp.where(lane == blk, jnp.sum(sc, axis=1, keepdims=True), sb)
        mx_scr[...] = mx
        sb_scr[...] = sb

    def apply_v_pages(c, ring):
        acc = acc_scr[...]
        for bb in range(pages_per_step // ppb):
            p = p_scr[c * (pages_per_step // ppb) + bb]
            acc = acc + _dot_nt(p, block_operand(vbuf, ring, bb))
        acc_scr[...] = acc

    def chunk(c, carry):
        g = slot * n_chunks + c
        ring = g % SAMPLE_RING_SLOTS

        @pl.when(g + lookahead < n_global)
        def _():
            page_copies(g + lookahead, start=True)

        page_copies(g, start=False)

        @pl.when((slot > 0) & (slot < n_seq))
        def _():
            score_k_pages(c, ring)
            apply_v_pages(c, ring)

        @pl.when(slot == 0)
        def _():
            score_k_pages(c, ring)

        @pl.when(slot == n_seq)
        def _():
            apply_v_pages(c, ring)

        return carry

    def write_output():
        vn = jnp.concatenate([vn_ref[0], jnp.zeros((128 - ds, ATTN_WIDTH), F32)], axis=0)
        full = acc_scr[...] + _dot(pown_scr[...].astype(BF16), vn.astype(BF16))
        full = full / l_scr[:, 0:1]
        l_i = lax.broadcasted_iota(jnp.int32, (ds, ATTN_WIDTH), 1)
        out = jnp.zeros((ds, ATTN_WIDTH), F32)
        for hh in range(N_HEADS):
            out = out + jnp.where(l_i // HEAD_DIM == hh, full[hh * ds:(hh + 1) * ds, :], 0.0)
        o_ref[0] = out

    def begin():
        pl.when(slot == 0)(prime)
        pl.when(slot > 0)(finalize)
        pl.when(slot < n_seq)(load_queries)

    def run_chunks(c0, c1):
        lax.fori_loop(c0, c1, chunk, 0)

    def end():
        pl.when(slot > 0)(write_output)

    return begin, run_chunks, end


def _sample_scratch(ds, n_pages, pps):
    rows = N_HEADS * ds
    n_blocks = n_pages * PAGE_SIZE // MOBA_BLOCK
    ring_buf = pltpu.VMEM((SAMPLE_RING_SLOTS, pps, N_HEADS, HEAD_DIM, PAGE_SIZE), F32)
    return [ring_buf, ring_buf,
            pltpu.SemaphoreType.DMA((SAMPLE_RING_SLOTS,)),
            pltpu.SemaphoreType.DMA((SAMPLE_RING_SLOTS,)),
            pltpu.VMEM((rows, ATTN_WIDTH), F32),
            pltpu.VMEM((n_blocks, rows, MOBA_BLOCK), F32),
            pltpu.VMEM((rows, 128), F32),
            pltpu.VMEM((rows, 128), F32),
            pltpu.VMEM((n_blocks, rows, MOBA_BLOCK), BF16),
            pltpu.VMEM((rows, 128), F32),
            pltpu.VMEM((rows, 128), F32),
            pltpu.VMEM((rows, ATTN_WIDTH), F32)]


def _moba_body(pt_ref, qs_ref, kn_ref, vn_ref, ck_hbm, cv_hbm, qp_ref, kp_ref, vp_ref, os_ref, op_ref, *scratch,
               n_seq, n_chunks, pages_per_step, n_prompt_steps, n_qtiles, n_sample_scratch):
    t = pl.program_id(0)
    begin, run_chunks, end = _sample_stream(
        t, pt_ref, qs_ref, kn_ref, vn_ref, ck_hbm, cv_hbm, os_ref, *scratch[:n_sample_scratch],
        n_seq=n_seq, n_chunks=n_chunks, pages_per_step=pages_per_step)
    stage1, stage2 = _prompt_attention_stages(t % n_qtiles, qp_ref, kp_ref, vp_ref, op_ref,
                                              *scratch[n_sample_scratch:])
    n_steps = max(n_seq + 1, n_prompt_steps)
    sample_part = (lambda f: f()) if n_steps == n_seq + 1 else (lambda f: pl.when(t <= n_seq)(f))
    prompt_part = (lambda f: f()) if n_steps == n_prompt_steps else (lambda f: pl.when(t < n_prompt_steps)(f))
    c1, c2 = n_chunks // 4, (5 * n_chunks) // 8

    def sample_head():
        begin()
        run_chunks(0, c1)

    def sample_middle():
        run_chunks(c1, c2)

    def sample_tail():
        run_chunks(c2, n_chunks)
        end()

    sample_part(sample_head)
    prompt_part(stage1)
    sample_part(sample_middle)
    prompt_part(stage2)
    sample_part(sample_tail)


def _moba(q_p, k_paged, v_paged, q_s, k_new, v_new, cache_k, cache_v, page_table):
    b, s, w = q_p.shape
    db, ds, _ = q_s.shape
    n_pages = page_table.shape[1]
    assert (n_pages * PAGE_SIZE) % MOBA_BLOCK == 0, "past length must be whole MoBA blocks"
    pps = SAMPLE_PAGES_PER_STEP
    n_chunks = n_pages // pps
    assert n_chunks * pps == n_pages and pps % (MOBA_BLOCK // PAGE_SIZE) == 0
    ck = jnp.transpose(cache_k, (0, 2, 3, 1))
    cv = jnp.transpose(cache_v, (0, 2, 3, 1))
    pt = page_table.reshape(-1)

    tq = MOBA_BLOCK
    heads = PROMPT_HEADS_PER_STEP
    lanes = heads * HEAD_DIM
    n_qt, n_hg = s // tq, w // lanes
    n_prompt = b * n_hg * n_qt
    n_steps = max(db + 1, n_prompt)

    def prompt_index(t):
        tp = jnp.minimum(t, n_prompt - 1)
        return tp // (n_hg * n_qt), (tp // n_qt) % n_hg, tp % n_qt

    def q_map(t, pt_ref):
        bi, hg, i = prompt_index(t)
        return (bi, i, hg)

    def kv_map(t, pt_ref):
        bi, hg, _ = prompt_index(t)
        return (bi, 0, hg, 0, 0)

    cur = pl.BlockSpec((1, ds, w), lambda t, pt_ref: (jnp.minimum(t, db - 1), 0, 0))
    prev = pl.BlockSpec((1, ds, w), lambda t, pt_ref: (jnp.clip(t - 1, 0, db - 1), 0, 0))
    hbm = pl.BlockSpec(memory_space=pl.ANY)
    q_spec = pl.BlockSpec((1, tq, lanes), q_map)
    kv_spec = pl.BlockSpec((1, s // PAGE_SIZE, heads, HEAD_DIM, PAGE_SIZE), kv_map)
    sample_scratch = _sample_scratch(ds, n_pages, pps)
    grid_spec = pltpu.PrefetchScalarGridSpec(
        num_scalar_prefetch=1,
        grid=(n_steps,),
        in_specs=[cur, prev, prev, hbm, hbm, q_spec, kv_spec, kv_spec],
        out_specs=[prev, q_spec],
        scratch_shapes=sample_scratch + _prompt_scratch(s, tq, heads))
    body = functools.partial(_moba_body, n_seq=db, n_chunks=n_chunks, pages_per_step=pps,
                             n_prompt_steps=n_prompt, n_qtiles=n_qt, n_sample_scratch=len(sample_scratch))
    y_s, y_p = pl.pallas_call(
        body,
        grid_spec=grid_spec,
        out_shape=[jax.ShapeDtypeStruct((db, ds, w), F32), jax.ShapeDtypeStruct((b, s, w), F32)],
        compiler_params=pltpu.CompilerParams(dimension_semantics=("arbitrary",),
                                             vmem_limit_bytes=VMEM_LIMIT),
        name="moba",
    )(pt, q_s, k_new, v_new, ck, cv, q_p, k_paged, v_paged)
    return y_p, y_s


def _mixer_out_body(x_ref, yc_ref, ya_ref, ag_ref, gmc_ref, gma_ref, gate_ref, wb_ref, wo_ref, fg_ref, o_ref):
    gb, tb, d = x_ref.shape
    m = gb * tb
    yc = yc_ref[...].reshape(m, CONV_CH).astype(BF16)
    ya = (ya_ref[...] * _silu(ag_ref[...])).reshape(m, ATTN_WIDTH).astype(BF16)
    merged = (jax.nn.sigmoid(gmc_ref[...].reshape(m, d)) * _dot(yc, wb_ref[0])
              + jax.nn.sigmoid(gma_ref[...].reshape(m, d)) * _dot(ya, wb_ref[1]))
    o = _dot(merged.astype(BF16), wo_ref[...]).reshape(gb, tb, d)
    xo = x_ref[...] + gate_ref[...] * o
    r = xo * lax.rsqrt(jnp.mean(xo * xo, axis=-1, keepdims=True) + EPS)
    o_ref[...] = r * fg_ref[...]


def _mixer_out(x3, y_conv, y_attn, ag, gmc, gma, gate, wb_bf, wo_bf, final_g, gb, tb):
    g, t, d = x3.shape
    xmap = lambda i, j: (i, j, 0)
    bmap = lambda i, j: (i, 0, 0)
    spec = lambda w: pl.BlockSpec((gb, tb, w), xmap)
    return pl.pallas_call(
        _mixer_out_body,
        grid=(g // gb, t // tb),
        in_specs=[spec(d), spec(CONV_CH), spec(ATTN_WIDTH), spec(ATTN_WIDTH), spec(d), spec(d),
                  pl.BlockSpec((gb, 1, d), bmap),
                  pl.BlockSpec((2, CONV_CH, d), lambda i, j: (0, 0, 0)),
                  pl.BlockSpec((d, d), lambda i, j: (0, 0)),
                  pl.BlockSpec((1, d), lambda i, j: (0, 0))],
        out_specs=spec(d),
        out_shape=jax.ShapeDtypeStruct((g, t, d), F32),
        compiler_params=pltpu.CompilerParams(dimension_semantics=("parallel", "parallel"),
                                             vmem_limit_bytes=VMEM_LIMIT),
        name="mixer_out",
    )(x3, y_conv, y_attn, ag, gmc, gma, gate, wb_bf, wo_bf, final_g)


def kernel(x_prompt, x_sample, cache_k, cache_v, state_conv, page_table, c_prompt, c_sample, norm_g, w_ada,
           b_ada, w_in, conv_w, conv_b, conv_ln_g, conv_ln_b, w_branch, w_out, final_g):
    depth = norm_g.shape[0]
    assert depth == 1, "single-layer trunk"
    b, s, d = x_prompt.shape
    db, ds, _ = x_sample.shape
    assert CONV_CH == ATTN_WIDTH and s % MOBA_BLOCK == 0 and s >= CONV_WIDTH - 1
    past_len = page_table.shape[1] * PAGE_SIZE
    lyr = 0

    ada = _ada(jnp.concatenate([c_prompt, c_sample], axis=0), w_ada[lyr], b_ada[lyr])
    shift, scale, gate = (ada[:, j * d:(j + 1) * d][:, None, :] for j in range(3))
    w_in_bf = w_in[lyr].astype(BF16)
    wb_bf = w_branch[lyr].astype(BF16)
    wo_bf = w_out[lyr].astype(BF16)
    g_in = norm_g[lyr].reshape(1, d)
    g_fin = final_g.reshape(1, d)
    cw, cb = conv_w[lyr], conv_b[lyr].reshape(1, CONV_CH)
    lg, lb = conv_ln_g[lyr].reshape(1, CONV_CH), conv_ln_b[lyr].reshape(1, CONV_CH)

    tab_p = _rope_tables(jnp.arange(s))
    yc_p, glu_tail, q_p, k_p, v_p, ag_p, gmc_p, gma_p = _mixer_in(
        x_prompt, scale[:b], shift[:b], g_in, w_in_bf, tab_p, 1, ROW_TILE, conv_params=(cw, cb, lg, lb))
    gbs = ROW_TILE // ds
    tab_s = tuple(jnp.tile(a, (gbs, 1)) for a in _rope_tables(past_len + jnp.arange(ds)))
    glu_s, cg_s, q_s, k_s, v_s, ag_s, gmc_s, gma_s = _mixer_in(
        x_sample, scale[b:], shift[b:], g_in, w_in_bf, tab_s, gbs, ds)
    hist = jnp.concatenate([state_conv[lyr], glu_s], axis=1)
    yc_s = _conv_sample(hist, cg_s, cw, cb, lg, lb)

    ya_p, ya_s = _moba(q_p, k_p, v_p, q_s, k_s, v_s, cache_k[lyr], cache_v[lyr], page_table)

    y_prompt = _mixer_out(x_prompt, yc_p, ya_p, ag_p, gmc_p, gma_p, gate[:b], wb_bf, wo_bf, g_fin, 1, OUT_ROW_TILE)
    y_sample = _mixer_out(x_sample, yc_s, ya_s, ag_s, gmc_s, gma_s, gate[b:], wb_bf, wo_bf, g_fin,
                          OUT_ROW_TILE // ds, ds)
    k_prompt = jnp.transpose(k_p, (0, 1, 4, 2, 3))[None]
    v_prompt = jnp.transpose(v_p, (0, 1, 4, 2, 3))[None]
    conv_prompt = glu_tail[None, :, HIST_PAD - (CONV_WIDTH - 1):, :]
    k_sample = k_s.reshape(1, db, ds, N_HEADS, HEAD_DIM)
    v_sample = v_s.reshape(1, db, ds, N_HEADS, HEAD_DIM)
    conv_sample = hist[None, :, ds:, :]

    return (y_prompt, y_sample, k_prompt, v_prompt, conv_prompt, k_sample, v_sample, conv_sample)
```

```python
import functools

import jax
import jax.numpy as jnp
from jax import lax
from jax.experimental import pallas as pl
from jax.experimental.pallas import tpu as pltpu

F32 = jnp.float32
BF16 = jnp.bfloat16

D_MODEL = 1024
HEAD_DIM = 64
N_HEADS = 8
ATTN_WIDTH = N_HEADS * HEAD_DIM
CONV_CH = 512
ROT_DIM = HEAD_DIM // 4
ROPE_THETA = 500000.0
MOBA_BLOCK = 256
MOBA_TOPK = 3
CONV_WIDTH = 31
PAGE_SIZE = 128
EPS = 1e-6
IN_SIZES = (CONV_CH, CONV_CH, CONV_CH, ATTN_WIDTH, ATTN_WIDTH, ATTN_WIDTH, ATTN_WIDTH, D_MODEL, D_MODEL)
IN_COLS = sum(IN_SIZES)
NEG = -1e30
SCORE_SCALE = HEAD_DIM ** -0.5 * 1.4426950408889634

ROPE_TABLE_LANES = 128
ROW_TILE = 256
OUT_ROW_TILE = 512
CONV_SUB = 32
HIST_PAD = 32
PROMPT_HEADS_PER_STEP = 4
SAMPLE_PAGES_PER_STEP = 8
SAMPLE_RING_SLOTS = 5
VMEM_LIMIT = 56 * 1024 * 1024


def _silu(x):
    return x * jax.nn.sigmoid(x)


def _dot(a, b):
    return jnp.dot(a, b, preferred_element_type=F32)


def _dot_nt(a, b):
    return lax.dot_general(a, b, (((1,), (1,)), ((), ())), preferred_element_type=F32)


def _dot_nt_split(a, b):
    a_hi, b_hi = a.astype(BF16), b.astype(BF16)
    a_lo = (a - a_hi.astype(F32)).astype(BF16)
    b_lo = (b - b_hi.astype(F32)).astype(BF16)
    return _dot_nt(a_hi, b_hi) + (_dot_nt(a_hi, b_lo) + _dot_nt(a_lo, b_hi))


def _ada_body(c_ref, w_ref, b_ref, o_ref):
    a = _silu(c_ref[...]).astype(BF16)
    o_ref[...] = _dot(a, w_ref[...].astype(BF16)) + b_ref[...]


def _ada(c_all, w_ada, b_ada):
    n = c_all.shape[0]
    return pl.pallas_call(
        _ada_body,
        grid=(3,),
        in_specs=[pl.BlockSpec((n, D_MODEL), lambda j: (0, 0)),
                  pl.BlockSpec((D_MODEL, D_MODEL), lambda j: (0, j)),
                  pl.BlockSpec((1, D_MODEL), lambda j: (0, j))],
        out_specs=pl.BlockSpec((n, D_MODEL), lambda j: (0, j)),
        out_shape=jax.ShapeDtypeStruct((n, 3 * D_MODEL), F32),
        compiler_params=pltpu.CompilerParams(dimension_semantics=("arbitrary",),
                                             vmem_limit_bytes=VMEM_LIMIT),
        name="ada",
    )(c_all, w_ada, b_ada.reshape(1, 3 * D_MODEL))


def _rope_rows(x, cos, sin_lo, sin_hi):
    n = x.shape[-1]
    half = ROT_DIM // 2
    return x * cos + pltpu.roll(x, n - half, 1) * sin_lo + pltpu.roll(x, half, 1) * sin_hi


def _mixer_in_body(*refs, prompt):
    x_ref, scale_ref, shift_ref, g_ref, w_ref, cos_ref, slo_ref, shi_ref = refs[:8]
    if prompt:
        cw_ref, cb_ref, lg_ref, lb_ref = refs[8:12]
        yconv_ref, tail_ref, q_ref, k_ref, v_ref, ag_ref, gmc_ref, gma_ref, hist_scr, shift_scr = refs[12:]
    else:
        glu_ref, cg_ref, q_ref, k_ref, v_ref, ag_ref, gmc_ref, gma_ref = refs[8:]
    gb, tb, d = x_ref.shape
    m = gb * tb
    if prompt:
        _carry_conv_history(pl.program_id(1) == 0, hist_scr, tb)
    x = x_ref[...]
    r = x * lax.rsqrt(jnp.mean(x * x, axis=-1, keepdims=True) + EPS)
    h = (r * g_ref[...]) * (1.0 + scale_ref[...]) + shift_ref[...]
    hb = h.reshape(m, d).astype(BF16)
    starts = [sum(IN_SIZES[:idx]) for idx in range(len(IN_SIZES))]
    half = D_MODEL // 2

    def cols(lo, width):
        return _dot(hb, w_ref[:, lo:lo + width])

    def put(ref, val):
        ref[...] = val.reshape(ref.shape)

    def put_half(ref, lo, part):
        ref[:, :, part * half:(part + 1) * half] = cols(lo + part * half, half).reshape(gb, tb, half)

    def put_kv(ref, val):
        if not prompt:
            return put(ref, val)
        val_t = val.T
        for pg in range(m // PAGE_SIZE):
            ref[0, pg] = val_t[:, pg * PAGE_SIZE:(pg + 1) * PAGE_SIZE].reshape(N_HEADS, HEAD_DIM, PAGE_SIZE)

    def rope(val):
        wide = lambda ref: jnp.concatenate([ref[...]] * (ATTN_WIDTH // ROPE_TABLE_LANES), axis=1)
        return _rope_rows(val, wide(cos_ref), wide(slo_ref), wide(shi_ref))

    glu = cols(starts[0], CONV_CH) * jax.nn.sigmoid(cols(starts[1], CONV_CH))
    cg = cols(starts[2], CONV_CH)
    pieces = [lambda: put(q_ref, rope(cols(starts[3], ATTN_WIDTH))),
              lambda: put_kv(k_ref, rope(cols(starts[4], ATTN_WIDTH))),
              lambda: put_kv(v_ref, cols(starts[5], ATTN_WIDTH)),
              lambda: put(ag_ref, cols(starts[6], ATTN_WIDTH)),
              lambda: put_half(gmc_ref, starts[7], 0), lambda: put_half(gmc_ref, starts[7], 1),
              lambda: put_half(gma_ref, starts[8], 0), lambda: put_half(gma_ref, starts[8], 1)]
    if prompt:
        conv_chunks = _causal_conv_tile(glu, cg, cw_ref, cb_ref, lg_ref, lb_ref, yconv_ref, tail_ref,
                                        hist_scr, shift_scr)
    else:
        put(glu_ref, glu)
        put(cg_ref, cg)
        conv_chunks = []
    per_piece = -(-len(conv_chunks) // len(pieces))
    for idx, piece in enumerate(pieces):
        piece()
        for chunk in conv_chunks[idx * per_piece:(idx + 1) * per_piece]:
            chunk()


def _mixer_in(x3, scale, shift, norm_g, w_in_bf, tables, gb, tb, conv_params=None):
    g, t, d = x3.shape
    m = gb * tb
    prompt = conv_params is not None
    grid = (g // gb, t // tb)
    xmap = lambda i, j: (i, j, 0)
    bmap = lambda i, j: (i, 0, 0)
    cmap = lambda i, j: (0, 0)
    tmap = lambda i, j: (j, 0)
    widths = (CONV_CH, CONV_CH, ATTN_WIDTH, ATTN_WIDTH, ATTN_WIDTH, ATTN_WIDTH, D_MODEL, D_MODEL)
    out_specs = [pl.BlockSpec((gb, tb, w), xmap) for w in widths]
    out_shape = [jax.ShapeDtypeStruct((g, t, w), F32) for w in widths]
    in_specs = [pl.BlockSpec((gb, tb, d), xmap),
                pl.BlockSpec((gb, 1, d), bmap),
                pl.BlockSpec((gb, 1, d), bmap),
                pl.BlockSpec((1, d), cmap),
                pl.BlockSpec((d, IN_COLS), cmap, pipeline_mode=pl.Buffered(1)),
                pl.BlockSpec((m, ROPE_TABLE_LANES), tmap),
                pl.BlockSpec((m, ROPE_TABLE_LANES), tmap),
                pl.BlockSpec((m, ROPE_TABLE_LANES), tmap)]
    operands = [x3, scale, shift, norm_g, w_in_bf, *tables]
    scratch = []
    if prompt:
        assert gb == 1 and tb % PAGE_SIZE == 0 and tb >= HIST_PAD
        out_shape[0] = jax.ShapeDtypeStruct((g, t, CONV_CH), BF16)
        out_specs[1] = pl.BlockSpec((1, HIST_PAD, CONV_CH), bmap)
        out_shape[1] = jax.ShapeDtypeStruct((g, HIST_PAD, CONV_CH), F32)
        for idx in (3, 4):
            out_specs[idx] = pl.BlockSpec((1, tb // PAGE_SIZE, N_HEADS, HEAD_DIM, PAGE_SIZE),
                                          lambda i, j: (i, j, 0, 0, 0))
            out_shape[idx] = jax.ShapeDtypeStruct((g, t // PAGE_SIZE, N_HEADS, HEAD_DIM, PAGE_SIZE), F32)
        in_specs += [pl.BlockSpec((CONV_WIDTH, CONV_CH), cmap)] + [pl.BlockSpec((1, CONV_CH), cmap)] * 3
        operands += list(conv_params)
        scratch = [pltpu.VMEM((HIST_PAD + tb, CONV_CH), F32),
                   pltpu.VMEM((7, _conv_shift_span(tb), CONV_CH), F32)]
    return pl.pallas_call(
        functools.partial(_mixer_in_body, prompt=prompt),
        grid=grid,
        in_specs=in_specs,
        out_specs=out_specs,
        out_shape=out_shape,
        scratch_shapes=scratch,
        compiler_params=pltpu.CompilerParams(
            dimension_semantics=("parallel", "arbitrary" if prompt else "parallel"),
            vmem_limit_bytes=VMEM_LIMIT),
        name="mixer_in",
    )(*operands)


def _rope_tables(pos):
    inv = ROPE_THETA ** (-(jnp.arange(0, ROT_DIM, 2, dtype=F32) / ROT_DIM))
    ang = pos.astype(F32)[:, None] * inv[None, :]
    cos, sin = jnp.cos(ang), jnp.sin(ang)
    half = ROT_DIM // 2
    n = pos.shape[0]
    pad = jnp.zeros((n, HEAD_DIM - ROT_DIM), F32)
    zero = jnp.zeros((n, half), F32)
    cos_h = jnp.concatenate([cos, cos, pad + 1.0], axis=1)
    slo_h = jnp.concatenate([-sin, zero, pad], axis=1)
    shi_h = jnp.concatenate([zero, sin, pad], axis=1)
    return tuple(jnp.tile(a, (1, ROPE_TABLE_LANES // HEAD_DIM)) for a in (cos_h, slo_h, shi_h))


def _conv_epilogue(y, cg, b, lg, lb):
    y = y + b
    mu = jnp.mean(y, axis=-1, keepdims=True)
    yc = y - mu
    var = jnp.mean(yc * yc, axis=-1, keepdims=True)
    z = _silu(yc * lax.rsqrt(var + EPS) * lg + lb)
    return z * _silu(cg)


def _conv_tap_offsets():
    off = HIST_PAD - (CONV_WIDTH - 1)
    return [divmod(off + tap, 8) for tap in range(CONV_WIDTH)]


def _conv_shift_span(tc):
    return tc + 8 * max(a for a, r in _conv_tap_offsets() if r)


def _carry_conv_history(first, hist_scr, tc):
    @pl.when(first)
    def _():
        hist_scr[0:HIST_PAD, :] = jnp.zeros((HIST_PAD, CONV_CH), F32)

    @pl.when(jnp.logical_not(first))
    def _():
        hist_scr[0:HIST_PAD, :] = hist_scr[tc:tc + HIST_PAD, :]


def _causal_conv_tile(glu, cg, w_ref, b_ref, lg_ref, lb_ref, o_ref, tail_ref, hist_scr, shift_scr):
    tc = glu.shape[0]
    hist_scr[HIST_PAD:HIST_PAD + tc, :] = glu
    tail_ref[0] = glu[tc - HIST_PAD:, :]
    span = _conv_shift_span(tc)
    for r in range(1, 8):
        shift_scr[r - 1] = hist_scr[r:r + span, :]

    def rows(r0):
        acc = jnp.zeros((CONV_SUB, CONV_CH), F32)
        for tap, (a, r) in enumerate(_conv_tap_offsets()):
            lo = r0 + 8 * a
            win = hist_scr[lo:lo + CONV_SUB, :] if r == 0 else shift_scr[r - 1, lo:lo + CONV_SUB, :]
            acc = acc + w_ref[tap:tap + 1, :] * win
        o_ref[0, r0:r0 + CONV_SUB, :] = _conv_epilogue(
            acc, cg[r0:r0 + CONV_SUB, :], b_ref[...], lg_ref[...], lb_ref[...]).astype(o_ref.dtype)

    return [functools.partial(rows, c * CONV_SUB) for c in range(tc // CONV_SUB)]


def _conv_sample_body(state_ref, glu_ref, cg_ref, w_ref, b_ref, lg_ref, lb_ref, o_ref, new_state_ref):
    n_hist = state_ref.shape[0]
    nb, ds, c = cg_ref.shape

    def hist(j):
        return state_ref[j] if j < n_hist else glu_ref[:, j - n_hist, :]

    for t in range(ds):
        acc = jnp.zeros((nb, c), F32)
        for tap in range(CONV_WIDTH):
            acc = acc + w_ref[tap:tap + 1, :] * hist(t + tap)
        o_ref[:, t, :] = _conv_epilogue(acc, cg_ref[:, t, :], b_ref[...], lg_ref[...], lb_ref[...])
    for j in range(n_hist):
        new_state_ref[j] = hist(j + ds)


def _conv_sample(state_t, glu, cg, conv_w, conv_b, ln_g, ln_b):
    n, t, c = cg.shape
    n_hist = state_t.shape[0]
    assert n_hist == CONV_WIDTH - 1
    nb = 8
    cmap = lambda i: (0, 0)
    rows = pl.BlockSpec((nb, t, c), lambda i: (i, 0, 0))
    state = pl.BlockSpec((n_hist, nb, c), lambda i: (0, i, 0))
    return pl.pallas_call(
        _conv_sample_body,
        grid=(n // nb,),
        in_specs=[state, rows, rows,
                  pl.BlockSpec((CONV_WIDTH, c), cmap),
                  pl.BlockSpec((1, c), cmap),
                  pl.BlockSpec((1, c), cmap),
                  pl.BlockSpec((1, c), cmap)],
        out_specs=[rows, state],
        out_shape=[jax.ShapeDtypeStruct((n, t, c), F32), jax.ShapeDtypeStruct((n_hist, n, c), F32)],
        compiler_params=pltpu.CompilerParams(dimension_semantics=("parallel",),
                                             vmem_limit_bytes=VMEM_LIMIT),
        name="conv_sample",
    )(state_t, glu, cg, conv_w, conv_b, ln_g, ln_b)


def _prompt_attention_stages(i, q_ref, k_ref, v_ref, o_ref, kb_scr, vt_scr, kmean_scr, bias_scr, s_scr, acc_scr,
                             pmx_scr):
    tq = q_ref.shape[1]
    n_pages, heads = k_ref.shape[1], k_ref.shape[2]
    ppb = MOBA_BLOCK // PAGE_SIZE
    nblk = n_pages // ppb

    def fold(x, op):
        return op(x.reshape(MOBA_BLOCK // 8, 8, tq), axis=0)

    def prepare():
        for blk in range(nblk):
            ksum = jnp.zeros((1, heads * HEAD_DIM), F32)
            for hp in range(ppb):
                pg = blk * ppb + hp
                k_rows = k_ref[0, pg].reshape(heads * HEAD_DIM, PAGE_SIZE).T
                ksum = ksum + jnp.sum(k_rows, axis=0, keepdims=True)
                for hh in range(heads):
                    kb_scr[hh, pg * PAGE_SIZE:(pg + 1) * PAGE_SIZE, :] = (
                        k_rows[:, hh * HEAD_DIM:(hh + 1) * HEAD_DIM].astype(BF16))
            kmean_scr[blk:blk + 1, :] = ksum * (1.0 / MOBA_BLOCK)
            for hh in range(heads):
                vt_scr[hh, blk] = jnp.concatenate(
                    [v_ref[0, blk * ppb + hp, hh] for hp in range(ppb)], axis=1).astype(BF16)

    def stage1():
        pl.when(i == 0)(prepare)
        kmean = kmean_scr[...]
        blk_iota = lax.broadcasted_iota(jnp.int32, (nblk, tq), 0)
        past = blk_iota < i
        causal = (lax.broadcasted_iota(jnp.int32, (MOBA_BLOCK, tq), 0)
                  <= lax.broadcasted_iota(jnp.int32, (MOBA_BLOCK, tq), 1))
        qbs = []
        for hh in range(heads):
            q_h = q_ref[0, :, hh * HEAD_DIM:(hh + 1) * HEAD_DIM]
            s_blk = _dot_nt_split(kmean[:, hh * HEAD_DIM:(hh + 1) * HEAD_DIM], q_h)
            s_blk = jnp.where(past, s_blk, -jnp.inf)
            rank = jnp.zeros((nblk, tq), jnp.int32)
            for mm in range(nblk):
                sm = s_blk[mm:mm + 1, :]
                rank = rank + jnp.where(sm > s_blk, 1, jnp.where((sm == s_blk) & (blk_iota > mm), 1, 0))
            bias_scr[hh] = jnp.where(past & (rank < MOBA_TOPK), 0.0, NEG)
            qbs.append((q_h * SCORE_SCALE).astype(BF16))

        def scores(hh, j):
            kj = kb_scr[hh, pl.ds(pl.multiple_of(j * MOBA_BLOCK, MOBA_BLOCK), MOBA_BLOCK), :]
            return _dot_nt(kj, qbs[hh])

        mx_own = []
        for hh in range(heads):
            s_t = jnp.where(causal, scores(hh, i), NEG)
            s_scr[hh, i] = s_t
            mx_own.append(fold(s_t, jnp.max))

        def score_blocks(jj, mx):
            out = list(mx)
            for j in (2 * jj, jnp.minimum(2 * jj + 1, i - 1)):
                for hh in range(heads):
                    s_t = scores(hh, j) + bias_scr[hh, pl.ds(j, 1), :]
                    s_scr[hh, j] = s_t
                    out[hh] = jnp.maximum(out[hh], fold(s_t, jnp.max))
            return tuple(out)

        mx = lax.fori_loop(0, (i + 1) // 2, score_blocks, tuple(mx_own))
        for hh in range(heads):
            pmx_scr[hh] = mx[hh]

    def stage2():
        m_fin = [jnp.max(pmx_scr[hh], axis=0, keepdims=True) for hh in range(heads)]
        for hh in range(heads):
            acc_scr[hh] = jnp.zeros((HEAD_DIM, tq), F32)

        def apply_blocks(jj, ls):
            out = list(ls)
            for j, live in ((2 * jj, True), (jnp.minimum(2 * jj + 1, i), 2 * jj + 1 <= i)):
                for hh in range(heads):
                    shift = m_fin[hh] if live is True else jnp.where(live, m_fin[hh], jnp.inf)
                    p = jnp.exp2(s_scr[hh, j] - shift)
                    out[hh] = out[hh] + fold(p, jnp.sum)
                    acc_scr[hh] = acc_scr[hh] + _dot(vt_scr[hh, j], p.astype(BF16))
            return tuple(out)

        ls = lax.fori_loop(0, i // 2 + 1, apply_blocks, tuple(jnp.zeros((8, tq), F32) for _ in range(heads)))
        o_ref[0] = jnp.concatenate(
            [acc_scr[hh] / jnp.sum(ls[hh], axis=0, keepdims=True) for hh in range(heads)], axis=0).T

    return stage1, stage2


def _prompt_scratch(s, tq, heads):
    nblk = s // MOBA_BLOCK
    return [pltpu.VMEM((heads, s, HEAD_DIM), BF16),
            pltpu.VMEM((heads, nblk, HEAD_DIM, MOBA_BLOCK), BF16),
            pltpu.VMEM((nblk, heads * HEAD_DIM), F32),
            pltpu.VMEM((heads, nblk, tq), F32),
            pltpu.VMEM((heads, nblk, MOBA_BLOCK, tq), F32),
            pltpu.VMEM((heads, HEAD_DIM, tq), F32),
            pltpu.VMEM((heads, 8, tq), F32)]


def _sample_stream(slot, pt_ref, q_ref, kn_ref, vn_ref, ck_hbm, cv_hbm, o_ref,
                   kbuf, vbuf, ksem, vsem, qbd_scr, s_scr, mx_scr, sb_scr, p_scr, pown_scr, l_scr, acc_scr,
                   *, n_seq, n_chunks, pages_per_step):
    ds = q_ref.shape[1]
    rows = N_HEADS * ds
    ppb = MOBA_BLOCK // PAGE_SIZE
    n_pages = n_chunks * pages_per_step
    n_blocks = n_pages // ppb
    n_global = (n_seq + 1) * n_chunks
    lookahead = SAMPLE_RING_SLOTS - 1
    lane = lax.broadcasted_iota(jnp.int32, (rows, 128), 1)
    row = lax.broadcasted_iota(jnp.int32, (rows, 128), 0)

    def page_copies(g, start):
        g_slot = g // n_chunks
        g_c = g % n_chunks
        ring = g % SAMPLE_RING_SLOTS

        def each(cache, buf, sem, seq):
            for p in range(pages_per_step):
                if start:
                    page = pt_ref[seq * n_pages + g_c * pages_per_step + p]
                    pltpu.make_async_copy(cache.at[page], buf.at[ring, p], sem.at[ring]).start()
                else:
                    pltpu.make_async_copy(cache.at[0], buf.at[ring, p], sem.at[ring]).wait()

        @pl.when(g_slot < n_seq)
        def _():
            each(ck_hbm, kbuf, ksem, g_slot)

        @pl.when(g_slot > 0)
        def _():
            each(cv_hbm, vbuf, vsem, g_slot - 1)

    def prime():
        for g in range(lookahead):
            page_copies(jnp.int32(g), start=True)

    def scaled_q():
        return (qbd_scr[...] * SCORE_SCALE).astype(BF16)

    def finalize():
        sc = jnp.where(lane < n_blocks, sb_scr[...], -jnp.inf)
        lane_f = lane.astype(F32)
        sel = jnp.zeros((rows, 128), F32)
        for _ in range(min(MOBA_TOPK, n_blocks)):
            top = jnp.max(sc, axis=1, keepdims=True)
            first = jnp.min(jnp.where(sc == top, lane_f, 1e9), axis=1, keepdims=True)
            pick = lane_f == first
            sel = jnp.where(pick, 1.0, sel)
            sc = jnp.where(pick, -jnp.inf, sc)
        kn = jnp.concatenate([kn_ref[0], jnp.zeros((128 - ds, ATTN_WIDTH), F32)], axis=0)
        s_own = jnp.where(lane <= row % ds, _dot_nt(scaled_q(), kn.astype(BF16)), NEG)
        m_sel = jnp.max(jnp.where(sel > 0, mx_scr[...], NEG), axis=1, keepdims=True)
        m_fin = jnp.maximum(m_sel, jnp.max(s_own, axis=1, keepdims=True))
        p_own = jnp.exp2(s_own - m_fin)
        pown_scr[...] = p_own
        p_sum = p_own
        for blk in range(n_blocks):
            shift = jnp.where(sel[:, blk:blk + 1] > 0, m_fin, jnp.inf)
            p = jnp.exp2(s_scr[blk] - shift)
            p_scr[blk] = p.astype(BF16)
            for hp in range(ppb):
                p_sum = p_sum + p[:, hp * PAGE_SIZE:(hp + 1) * PAGE_SIZE]
        l_scr[...] = jnp.broadcast_to(jnp.sum(p_sum, axis=1, keepdims=True), l_scr.shape)
        acc_scr[...] = jnp.zeros(acc_scr.shape, F32)

    def load_queries():
        q = q_ref[0]
        r_i = lax.broadcasted_iota(jnp.int32, (rows, ATTN_WIDTH), 0)
        l_i = lax.broadcasted_iota(jnp.int32, (rows, ATTN_WIDTH), 1)
        qbd_scr[...] = jnp.where(r_i // ds == l_i // HEAD_DIM, jnp.concatenate([q] * N_HEADS, axis=0), 0.0)
        mx_scr[...] = jnp.full(mx_scr.shape, NEG, F32)
        sb_scr[...] = jnp.zeros(sb_scr.shape, F32)

    def block_operand(buf, ring, bb):
        return jnp.concatenate(
            [buf[ring, bb * ppb + hp].reshape(ATTN_WIDTH, PAGE_SIZE).astype(BF16) for hp in range(ppb)], axis=1)

    def score_k_pages(c, ring):
        qb = scaled_q()
        mx = mx_scr[...]
        sb = sb_scr[...]
        for bb in range(pages_per_step // ppb):
            blk = c * (pages_per_step // ppb) + bb
            sc = _dot(qb, block_operand(kbuf, ring, bb))
            s_scr[blk] = sc
            mx = jnp.where(lane == blk, jnp.max(sc, axis=1, keepdims=True), mx)
            sb = jnp.where(lane == blk, jnp.sum(sc, axis=1, keepdims=True), sb)
        mx_scr[...] = mx
        sb_scr[...] = sb

    def apply_v_pages(c, ring):
        acc = acc_scr[...]
        for bb in range(pages_per_step // ppb):
            p = p_scr[c * (pages_per_step // ppb) + bb]
            acc = acc + _dot_nt(p, block_operand(vbuf, ring, bb))
        acc_scr[...] = acc

    def chunk(c, carry):
        g = slot * n_chunks + c
        ring = g % SAMPLE_RING_SLOTS

        @pl.when(g + lookahead < n_global)
        def _():
            page_copies(g + lookahead, start=True)

        page_copies(g, start=False)

        @pl.when((slot > 0) & (slot < n_seq))
        def _():
            score_k_pages(c, ring)
            apply_v_pages(c, ring)

        @pl.when(slot == 0)
        def _():
            score_k_pages(c, ring)

        @pl.when(slot == n_seq)
        def _():
            apply_v_pages(c, ring)

        return carry

    def write_output():
        vn = jnp.concatenate([vn_ref[0], jnp.zeros((128 - ds, ATTN_WIDTH), F32)], axis=0)
        full = acc_scr[...] + _dot(pown_scr[...].astype(BF16), vn.astype(BF16))
        full = full / l_scr[:, 0:1]
        l_i = lax.broadcasted_iota(jnp.int32, (ds, ATTN_WIDTH), 1)
        out = jnp.zeros((ds, ATTN_WIDTH), F32)
        for hh in range(N_HEADS):
            out = out + jnp.where(l_i // HEAD_DIM == hh, full[hh * ds:(hh + 1) * ds, :], 0.0)
        o_ref[0] = out

    def begin():
        pl.when(slot == 0)(prime)
        pl.when(slot > 0)(finalize)
        pl.when(slot < n_seq)(load_queries)

    def run_chunks(c0, c1):
        lax.fori_loop(c0, c1, chunk, 0)

    def end():
        pl.when(slot > 0)(write_output)

    return begin, run_chunks, end


def _sample_scratch(ds, n_pages, pps):
    rows = N_HEADS * ds
    n_blocks = n_pages * PAGE_SIZE // MOBA_BLOCK
    ring_buf = pltpu.VMEM((SAMPLE_RING_SLOTS, pps, N_HEADS, HEAD_DIM, PAGE_SIZE), F32)
    return [ring_buf, ring_buf,
            pltpu.SemaphoreType.DMA((SAMPLE_RING_SLOTS,)),
            pltpu.SemaphoreType.DMA((SAMPLE_RING_SLOTS,)),
            pltpu.VMEM((rows, ATTN_WIDTH), F32),
            pltpu.VMEM((n_blocks, rows, MOBA_BLOCK), F32),
            pltpu.VMEM((rows, 128), F32),
            pltpu.VMEM((rows, 128), F32),
            pltpu.VMEM((n_blocks, rows, MOBA_BLOCK), BF16),
            pltpu.VMEM((rows, 128), F32),
            pltpu.VMEM((rows, 128), F32),
            pltpu.VMEM((rows, ATTN_WIDTH), F32)]


def _moba_body(pt_ref, qs_ref, kn_ref, vn_ref, ck_hbm, cv_hbm, qp_ref, kp_ref, vp_ref, os_ref, op_ref, *scratch,
               n_seq, n_chunks, pages_per_step, n_prompt_steps, n_qtiles, n_sample_scratch):
    t = pl.program_id(0)
    begin, run_chunks, end = _sample_stream(
        t, pt_ref, qs_ref, kn_ref, vn_ref, ck_hbm, cv_hbm, os_ref, *scratch[:n_sample_scratch],
        n_seq=n_seq, n_chunks=n_chunks, pages_per_step=pages_per_step)
    stage1, stage2 = _prompt_attention_stages(t % n_qtiles, qp_ref, kp_ref, vp_ref, op_ref,
                                              *scratch[n_sample_scratch:])
    n_steps = max(n_seq + 1, n_prompt_steps)
    sample_part = (lambda f: f()) if n_steps == n_seq + 1 else (lambda f: pl.when(t <= n_seq)(f))
    prompt_part = (lambda f: f()) if n_steps == n_prompt_steps else (lambda f: pl.when(t < n_prompt_steps)(f))
    c1, c2 = n_chunks // 4, (5 * n_chunks) // 8

    def sample_head():
        begin()
        run_chunks(0, c1)

    def sample_middle():
        run_chunks(c1, c2)

    def sample_tail():
        run_chunks(c2, n_chunks)
        end()

    sample_part(sample_head)
    prompt_part(stage1)
    sample_part(sample_middle)
    prompt_part(stage2)
    sample_part(sample_tail)


def _moba(q_p, k_paged, v_paged, q_s, k_new, v_new, cache_k, cache_v, page_table):
    b, s, w = q_p.shape
    db, ds, _ = q_s.shape
    n_pages = page_table.shape[1]
    assert (n_pages * PAGE_SIZE) % MOBA_BLOCK == 0, "past length must be whole MoBA blocks"
    pps = SAMPLE_PAGES_PER_STEP
    n_chunks = n_pages // pps
    assert n_chunks * pps == n_pages and pps % (MOBA_BLOCK // PAGE_SIZE) == 0
    ck = jnp.transpose(cache_k, (0, 2, 3, 1))
    cv = jnp.transpose(cache_v, (0, 2, 3, 1))
    pt = page_table.reshape(-1)

    tq = MOBA_BLOCK
    heads = PROMPT_HEADS_PER_STEP
    lanes = heads * HEAD_DIM
    n_qt, n_hg = s // tq, w // lanes
    n_prompt = b * n_hg * n_qt
    n_steps = max(db + 1, n_prompt)

    def prompt_index(t):
        tp = jnp.minimum(t, n_prompt - 1)
        return tp // (n_hg * n_qt), (tp // n_qt) % n_hg, tp % n_qt

    def q_map(t, pt_ref):
        bi, hg, i = prompt_index(t)
        return (bi, i, hg)

    def kv_map(t, pt_ref):
        bi, hg, _ = prompt_index(t)
        return (bi, 0, hg, 0, 0)

    cur = pl.BlockSpec((1, ds, w), lambda t, pt_ref: (jnp.minimum(t, db - 1), 0, 0))
    prev = pl.BlockSpec((1, ds, w), lambda t, pt_ref: (jnp.clip(t - 1, 0, db - 1), 0, 0))
    hbm = pl.BlockSpec(memory_space=pl.ANY)
    q_spec = pl.BlockSpec((1, tq, lanes), q_map)
    kv_spec = pl.BlockSpec((1, s // PAGE_SIZE, heads, HEAD_DIM, PAGE_SIZE), kv_map)
    sample_scratch = _sample_scratch(ds, n_pages, pps)
    grid_spec = pltpu.PrefetchScalarGridSpec(
        num_scalar_prefetch=1,
        grid=(n_steps,),
        in_specs=[cur, prev, prev, hbm, hbm, q_spec, kv_spec, kv_spec],
        out_specs=[prev, q_spec],
        scratch_shapes=sample_scratch + _prompt_scratch(s, tq, heads))
    body = functools.partial(_moba_body, n_seq=db, n_chunks=n_chunks, pages_per_step=pps,
                             n_prompt_steps=n_prompt, n_qtiles=n_qt, n_sample_scratch=len(sample_scratch))
    y_s, y_p = pl.pallas_call(
        body,
        grid_spec=grid_spec,
        out_shape=[jax.ShapeDtypeStruct((db, ds, w), F32), jax.ShapeDtypeStruct((b, s, w), F32)],
        compiler_params=pltpu.CompilerParams(dimension_semantics=("arbitrary",),
                                             vmem_limit_bytes=VMEM_LIMIT),
        name="moba",
    )(pt, q_s, k_new, v_new, ck, cv, q_p, k_paged, v_paged)
    return y_p, y_s


def _mixer_out_body(x_ref, yc_ref, ya_ref, ag_ref, gmc_ref, gma_ref, gate_ref, wb_ref, wo_ref, fg_ref, o_ref):
    gb, tb, d = x_ref.shape
    m = gb * tb
    yc = yc_ref[...].reshape(m, CONV_CH).astype(BF16)
    ya = (ya_ref[...] * _silu(ag_ref[...])).reshape(m, ATTN_WIDTH).astype(BF16)
    merged = (jax.nn.sigmoid(gmc_ref[...].reshape(m, d)) * _dot(yc, wb_ref[0])
              + jax.nn.sigmoid(gma_ref[...].reshape(m, d)) * _dot(ya, wb_ref[1]))
    o = _dot(merged.astype(BF16), wo_ref[...]).reshape(gb, tb, d)
    xo = x_ref[...] + gate_ref[...] * o
    r = xo * lax.rsqrt(jnp.mean(xo * xo, axis=-1, keepdims=True) + EPS)
    o_ref[...] = r * fg_ref[...]


def _mixer_out(x3, y_conv, y_attn, ag, gmc, gma, gate, wb_bf, wo_bf, final_g, gb, tb):
    g, t, d = x3.shape
    xmap = lambda i, j: (i, j, 0)
    bmap = lambda i, j: (i, 0, 0)
    spec = lambda w: pl.BlockSpec((gb, tb, w), xmap)
    return pl.pallas_call(
        _mixer_out_body,
        grid=(g // gb, t // tb),
        in_specs=[spec(d), spec(CONV_CH), spec(ATTN_WIDTH), spec(ATTN_WIDTH), spec(d), spec(d),
                  pl.BlockSpec((gb, 1, d), bmap),
                  pl.BlockSpec((2, CONV_CH, d), lambda i, j: (0, 0, 0)),
                  pl.BlockSpec((d, d), lambda i, j: (0, 0)),
                  pl.BlockSpec((1, d), lambda i, j: (0, 0))],
        out_specs=spec(d),
        out_shape=jax.ShapeDtypeStruct((g, t, d), F32),
        compiler_params=pltpu.CompilerParams(dimension_semantics=("parallel", "parallel"),
                                             vmem_limit_bytes=VMEM_LIMIT),
        name="mixer_out",
    )(x3, y_conv, y_attn, ag, gmc, gma, gate, wb_bf, wo_bf, final_g)


def kernel(x_prompt, x_sample, cache_k, cache_v, state_conv, page_table, c_prompt, c_sample, norm_g, w_ada,
           b_ada, w_in, conv_w, conv_b, conv_ln_g, conv_ln_b, w_branch, w_out, final_g):
    depth = norm_g.shape[0]
    assert depth == 1, "single-layer trunk"
    b, s, d = x_prompt.shape
    db, ds, _ = x_sample.shape
    assert CONV_CH == ATTN_WIDTH and s % MOBA_BLOCK == 0 and s >= CONV_WIDTH - 1
    past_len = page_table.shape[1] * PAGE_SIZE
    lyr = 0

    ada = _ada(jnp.concatenate([c_prompt, c_sample], axis=0), w_ada[lyr], b_ada[lyr])
    shift, scale, gate = (ada[:, j * d:(j + 1) * d][:, None, :] for j in range(3))
    w_in_bf = w_in[lyr].astype(BF16)
    wb_bf = w_branch[lyr].astype(BF16)
    wo_bf = w_out[lyr].astype(BF16)
    g_in = norm_g[lyr].reshape(1, d)
    g_fin = final_g.reshape(1, d)
    cw, cb = conv_w[lyr], conv_b[lyr].reshape(1, CONV_CH)
    lg, lb = conv_ln_g[lyr].reshape(1, CONV_CH), conv_ln_b[lyr].reshape(1, CONV_CH)

    tab_p = _rope_tables(jnp.arange(s))
    yc_p, glu_tail, q_p, k_p, v_p, ag_p, gmc_p, gma_p = _mixer_in(
        x_prompt, scale[:b], shift[:b], g_in, w_in_bf, tab_p, 1, ROW_TILE, conv_params=(cw, cb, lg, lb))
    gbs = ROW_TILE // ds
    tab_s = tuple(jnp.tile(a, (gbs, 1)) for a in _rope_tables(past_len + jnp.arange(ds)))
    glu_s, cg_s, q_s, k_s, v_s, ag_s, gmc_s, gma_s = _mixer_in(
        x_sample, scale[b:], shift[b:], g_in, w_in_bf, tab_s, gbs, ds)
    yc_s, state_next = _conv_sample(jnp.transpose(state_conv[lyr], (1, 0, 2)), glu_s, cg_s, cw, cb, lg, lb)

    ya_p, ya_s = _moba(q_p, k_p, v_p, q_s, k_s, v_s, cache_k[lyr], cache_v[lyr], page_table)

    y_prompt = _mixer_out(x_prompt, yc_p, ya_p, ag_p, gmc_p, gma_p, gate[:b], wb_bf, wo_bf, g_fin, 1, OUT_ROW_TILE)
    y_sample = _mixer_out(x_sample, yc_s, ya_s, ag_s, gmc_s, gma_s, gate[b:], wb_bf, wo_bf, g_fin,
                          OUT_ROW_TILE // ds, ds)
    k_prompt = jnp.transpose(k_p, (0, 1, 4, 2, 3))[None]
    v_prompt = jnp.transpose(v_p, (0, 1, 4, 2, 3))[None]
    conv_prompt = glu_tail[None, :, HIST_PAD - (CONV_WIDTH - 1):, :]
    k_sample = k_s.reshape(1, db, ds, N_HEADS, HEAD_DIM)
    v_sample = v_s.reshape(1, db, ds, N_HEADS, HEAD_DIM)
    conv_sample = jnp.transpose(state_next, (1, 0, 2))[None]

    return (y_prompt, y_sample, k_prompt, v_prompt, conv_prompt, k_sample, v_sample, conv_sample)
```

```python
import functools

import jax
import jax.numpy as jnp
from jax import lax
from jax.experimental import pallas as pl
from jax.experimental.pallas import tpu as pltpu

F32 = jnp.float32
BF16 = jnp.bfloat16

D_MODEL = 1024
HEAD_DIM = 64
N_HEADS = 8
ATTN_WIDTH = N_HEADS * HEAD_DIM
CONV_CH = 512
ROT_DIM = HEAD_DIM // 4
ROPE_THETA = 500000.0
MOBA_BLOCK = 256
MOBA_TOPK = 3
CONV_WIDTH = 31
PAGE_SIZE = 128
EPS = 1e-6
IN_SIZES = (CONV_CH, CONV_CH, CONV_CH, ATTN_WIDTH, ATTN_WIDTH, ATTN_WIDTH, ATTN_WIDTH, D_MODEL, D_MODEL)
IN_COLS = sum(IN_SIZES)
NEG = -1e30
SCORE_SCALE = HEAD_DIM ** -0.5 * 1.4426950408889634

ROPE_TABLE_LANES = 128
ROW_TILE = 256
OUT_ROW_TILE = 512
CONV_SUB = 32
HIST_PAD = 32
PROMPT_HEADS_PER_STEP = 4
SAMPLE_PAGES_PER_STEP = 16
SAMPLE_RING_SLOTS = 3
VMEM_LIMIT = 56 * 1024 * 1024


def _silu(x):
    return x * jax.nn.sigmoid(x)


def _dot(a, b):
    return jnp.dot(a, b, preferred_element_type=F32)


def _dot_nt(a, b):
    return lax.dot_general(a, b, (((1,), (1,)), ((), ())), preferred_element_type=F32)


def _dot_nt_split(a, b):
    a_hi, b_hi = a.astype(BF16), b.astype(BF16)
    a_lo = (a - a_hi.astype(F32)).astype(BF16)
    b_lo = (b - b_hi.astype(F32)).astype(BF16)
    return _dot_nt(a_hi, b_hi) + (_dot_nt(a_hi, b_lo) + _dot_nt(a_lo, b_hi))


def _ada_body(c_ref, w_ref, b_ref, o_ref):
    a = _silu(c_ref[...]).astype(BF16)
    o_ref[...] = _dot(a, w_ref[...].astype(BF16)) + b_ref[...]


def _ada(c_all, w_ada, b_ada):
    n = c_all.shape[0]
    return pl.pallas_call(
        _ada_body,
        grid=(3,),
        in_specs=[pl.BlockSpec((n, D_MODEL), lambda j: (0, 0)),
                  pl.BlockSpec((D_MODEL, D_MODEL), lambda j: (0, j)),
                  pl.BlockSpec((1, D_MODEL), lambda j: (0, j))],
        out_specs=pl.BlockSpec((n, D_MODEL), lambda j: (0, j)),
        out_shape=jax.ShapeDtypeStruct((n, 3 * D_MODEL), F32),
        compiler_params=pltpu.CompilerParams(dimension_semantics=("arbitrary",),
                                             vmem_limit_bytes=VMEM_LIMIT),
        name="ada",
    )(c_all, w_ada, b_ada.reshape(1, 3 * D_MODEL))


def _rope_rows(x, cos, sin_lo, sin_hi):
    n = x.shape[-1]
    half = ROT_DIM // 2
    return x * cos + pltpu.roll(x, n - half, 1) * sin_lo + pltpu.roll(x, half, 1) * sin_hi


def _mixer_in_body(*refs, prompt):
    x_ref, scale_ref, shift_ref, g_ref, w_ref, cos_ref, slo_ref, shi_ref = refs[:8]
    if prompt:
        cw_ref, cb_ref, lg_ref, lb_ref = refs[8:12]
        yconv_ref, tail_ref, q_ref, k_ref, v_ref, ag_ref, gmc_ref, gma_ref, hist_scr, shift_scr = refs[12:]
    else:
        glu_ref, cg_ref, q_ref, k_ref, v_ref, ag_ref, gmc_ref, gma_ref = refs[8:]
    gb, tb, d = x_ref.shape
    m = gb * tb
    if prompt:
        _carry_conv_history(pl.program_id(1) == 0, hist_scr, tb)
    x = x_ref[...]
    r = x * lax.rsqrt(jnp.mean(x * x, axis=-1, keepdims=True) + EPS)
    h = (r * g_ref[...]) * (1.0 + scale_ref[...]) + shift_ref[...]
    hb = h.reshape(m, d).astype(BF16)
    starts = [sum(IN_SIZES[:idx]) for idx in range(len(IN_SIZES))]
    half = D_MODEL // 2

    def cols(lo, width):
        return _dot(hb, w_ref[:, lo:lo + width])

    def put(ref, val):
        ref[...] = val.reshape(ref.shape)

    def put_half(ref, lo, part):
        ref[:, :, part * half:(part + 1) * half] = cols(lo + part * half, half).reshape(gb, tb, half)

    def put_kv(ref, val):
        if not prompt:
            return put(ref, val)
        val_t = val.T
        for pg in range(m // PAGE_SIZE):
            ref[0, pg] = val_t[:, pg * PAGE_SIZE:(pg + 1) * PAGE_SIZE].reshape(N_HEADS, HEAD_DIM, PAGE_SIZE)

    def rope(val):
        wide = lambda ref: jnp.concatenate([ref[...]] * (ATTN_WIDTH // ROPE_TABLE_LANES), axis=1)
        return _rope_rows(val, wide(cos_ref), wide(slo_ref), wide(shi_ref))

    glu = cols(starts[0], CONV_CH) * jax.nn.sigmoid(cols(starts[1], CONV_CH))
    cg = cols(starts[2], CONV_CH)
    pieces = [lambda: put(q_ref, rope(cols(starts[3], ATTN_WIDTH))),
              lambda: put_kv(k_ref, rope(cols(starts[4], ATTN_WIDTH))),
              lambda: put_kv(v_ref, cols(starts[5], ATTN_WIDTH)),
              lambda: put(ag_ref, cols(starts[6], ATTN_WIDTH)),
              lambda: put_half(gmc_ref, starts[7], 0), lambda: put_half(gmc_ref, starts[7], 1),
              lambda: put_half(gma_ref, starts[8], 0), lambda: put_half(gma_ref, starts[8], 1)]
    if prompt:
        conv_chunks = _causal_conv_tile(glu, cg, cw_ref, cb_ref, lg_ref, lb_ref, yconv_ref, tail_ref,
                                        hist_scr, shift_scr)
    else:
        put(glu_ref, glu)
        put(cg_ref, cg)
        conv_chunks = []
    per_piece = -(-len(conv_chunks) // len(pieces))
    for idx, piece in enumerate(pieces):
        piece()
        for chunk in conv_chunks[idx * per_piece:(idx + 1) * per_piece]:
            chunk()


def _mixer_in(x3, scale, shift, norm_g, w_in_bf, tables, gb, tb, conv_params=None):
    g, t, d = x3.shape
    m = gb * tb
    prompt = conv_params is not None
    grid = (g // gb, t // tb)
    xmap = lambda i, j: (i, j, 0)
    bmap = lambda i, j: (i, 0, 0)
    cmap = lambda i, j: (0, 0)
    tmap = lambda i, j: (j, 0)
    widths = (CONV_CH, CONV_CH, ATTN_WIDTH, ATTN_WIDTH, ATTN_WIDTH, ATTN_WIDTH, D_MODEL, D_MODEL)
    out_specs = [pl.BlockSpec((gb, tb, w), xmap) for w in widths]
    out_shape = [jax.ShapeDtypeStruct((g, t, w), F32) for w in widths]
    in_specs = [pl.BlockSpec((gb, tb, d), xmap),
                pl.BlockSpec((gb, 1, d), bmap),
                pl.BlockSpec((gb, 1, d), bmap),
                pl.BlockSpec((1, d), cmap),
                pl.BlockSpec((d, IN_COLS), cmap, pipeline_mode=pl.Buffered(1)),
                pl.BlockSpec((m, ROPE_TABLE_LANES), tmap),
                pl.BlockSpec((m, ROPE_TABLE_LANES), tmap),
                pl.BlockSpec((m, ROPE_TABLE_LANES), tmap)]
    operands = [x3, scale, shift, norm_g, w_in_bf, *tables]
    scratch = []
    if prompt:
        assert gb == 1 and tb % PAGE_SIZE == 0 and tb >= HIST_PAD
        out_shape[0] = jax.ShapeDtypeStruct((g, t, CONV_CH), BF16)
        out_specs[1] = pl.BlockSpec((1, HIST_PAD, CONV_CH), bmap)
        out_shape[1] = jax.ShapeDtypeStruct((g, HIST_PAD, CONV_CH), F32)
        for idx in (3, 4):
            out_specs[idx] = pl.BlockSpec((1, tb // PAGE_SIZE, N_HEADS, HEAD_DIM, PAGE_SIZE),
                                          lambda i, j: (i, j, 0, 0, 0))
            out_shape[idx] = jax.ShapeDtypeStruct((g, t // PAGE_SIZE, N_HEADS, HEAD_DIM, PAGE_SIZE), F32)
        in_specs += [pl.BlockSpec((CONV_WIDTH, CONV_CH), cmap)] + [pl.BlockSpec((1, CONV_CH), cmap)] * 3
        operands += list(conv_params)
        scratch = [pltpu.VMEM((HIST_PAD + tb, CONV_CH), F32),
                   pltpu.VMEM((7, _conv_shift_span(tb), CONV_CH), F32)]
    return pl.pallas_call(
        functools.partial(_mixer_in_body, prompt=prompt),
        grid=grid,
        in_specs=in_specs,
        out_specs=out_specs,
        out_shape=out_shape,
        scratch_shapes=scratch,
        compiler_params=pltpu.CompilerParams(
            dimension_semantics=("parallel", "arbitrary" if prompt else "parallel"),
            vmem_limit_bytes=VMEM_LIMIT),
        name="mixer_in",
    )(*operands)


def _rope_tables(pos):
    inv = ROPE_THETA ** (-(jnp.arange(0, ROT_DIM, 2, dtype=F32) / ROT_DIM))
    ang = pos.astype(F32)[:, None] * inv[None, :]
    cos, sin = jnp.cos(ang), jnp.sin(ang)
    half = ROT_DIM // 2
    n = pos.shape[0]
    pad = jnp.zeros((n, HEAD_DIM - ROT_DIM), F32)
    zero = jnp.zeros((n, half), F32)
    cos_h = jnp.concatenate([cos, cos, pad + 1.0], axis=1)
    slo_h = jnp.concatenate([-sin, zero, pad], axis=1)
    shi_h = jnp.concatenate([zero, sin, pad], axis=1)
    return tuple(jnp.tile(a, (1, ROPE_TABLE_LANES // HEAD_DIM)) for a in (cos_h, slo_h, shi_h))


def _conv_epilogue(y, cg, b, lg, lb):
    y = y + b
    mu = jnp.mean(y, axis=-1, keepdims=True)
    yc = y - mu
    var = jnp.mean(yc * yc, axis=-1, keepdims=True)
    z = _silu(yc * lax.rsqrt(var + EPS) * lg + lb)
    return z * _silu(cg)


def _conv_tap_offsets():
    off = HIST_PAD - (CONV_WIDTH - 1)
    return [divmod(off + tap, 8) for tap in range(CONV_WIDTH)]


def _conv_shift_span(tc):
    return tc + 8 * max(a for a, r in _conv_tap_offsets() if r)


def _carry_conv_history(first, hist_scr, tc):
    @pl.when(first)
    def _():
        hist_scr[0:HIST_PAD, :] = jnp.zeros((HIST_PAD, CONV_CH), F32)

    @pl.when(jnp.logical_not(first))
    def _():
        hist_scr[0:HIST_PAD, :] = hist_scr[tc:tc + HIST_PAD, :]


def _causal_conv_tile(glu, cg, w_ref, b_ref, lg_ref, lb_ref, o_ref, tail_ref, hist_scr, shift_scr):
    tc = glu.shape[0]
    hist_scr[HIST_PAD:HIST_PAD + tc, :] = glu
    tail_ref[0] = glu[tc - HIST_PAD:, :]
    span = _conv_shift_span(tc)
    for r in range(1, 8):
        shift_scr[r - 1] = hist_scr[r:r + span, :]

    def rows(r0):
        acc = jnp.zeros((CONV_SUB, CONV_CH), F32)
        for tap, (a, r) in enumerate(_conv_tap_offsets()):
            lo = r0 + 8 * a
            win = hist_scr[lo:lo + CONV_SUB, :] if r == 0 else shift_scr[r - 1, lo:lo + CONV_SUB, :]
            acc = acc + w_ref[tap:tap + 1, :] * win
        o_ref[0, r0:r0 + CONV_SUB, :] = _conv_epilogue(
            acc, cg[r0:r0 + CONV_SUB, :], b_ref[...], lg_ref[...], lb_ref[...]).astype(o_ref.dtype)

    return [functools.partial(rows, c * CONV_SUB) for c in range(tc // CONV_SUB)]


def _conv_sample_body(state_ref, glu_ref, cg_ref, w_ref, b_ref, lg_ref, lb_ref, o_ref, new_state_ref):
    n_hist = state_ref.shape[0]
    nb, ds, c = cg_ref.shape

    def hist(j):
        return state_ref[j] if j < n_hist else glu_ref[:, j - n_hist, :]

    for t in range(ds):
        acc = jnp.zeros((nb, c), F32)
        for tap in range(CONV_WIDTH):
            acc = acc + w_ref[tap:tap + 1, :] * hist(t + tap)
        o_ref[:, t, :] = _conv_epilogue(acc, cg_ref[:, t, :], b_ref[...], lg_ref[...], lb_ref[...])
    for j in range(n_hist):
        new_state_ref[j] = hist(j + ds)


def _conv_sample(state_t, glu, cg, conv_w, conv_b, ln_g, ln_b):
    n, t, c = cg.shape
    n_hist = state_t.shape[0]
    assert n_hist == CONV_WIDTH - 1
    nb = 8
    cmap = lambda i: (0, 0)
    rows = pl.BlockSpec((nb, t, c), lambda i: (i, 0, 0))
    state = pl.BlockSpec((n_hist, nb, c), lambda i: (0, i, 0))
    return pl.pallas_call(
        _conv_sample_body,
        grid=(n // nb,),
        in_specs=[state, rows, rows,
                  pl.BlockSpec((CONV_WIDTH, c), cmap),
                  pl.BlockSpec((1, c), cmap),
                  pl.BlockSpec((1, c), cmap),
                  pl.BlockSpec((1, c), cmap)],
        out_specs=[rows, state],
        out_shape=[jax.ShapeDtypeStruct((n, t, c), F32), jax.ShapeDtypeStruct((n_hist, n, c), F32)],
        compiler_params=pltpu.CompilerParams(dimension_semantics=("parallel",),
                                             vmem_limit_bytes=VMEM_LIMIT),
        name="conv_sample",
    )(state_t, glu, cg, conv_w, conv_b, ln_g, ln_b)


def _prompt_attention_stages(i, q_ref, k_ref, v_ref, o_ref, kb_scr, vt_scr, kmean_scr, bias_scr, s_scr, acc_scr,
                             pmx_scr):
    tq = q_ref.shape[1]
    n_pages, heads = k_ref.shape[1], k_ref.shape[2]
    ppb = MOBA_BLOCK // PAGE_SIZE
    nblk = n_pages // ppb

    def fold(x, op):
        return op(x.reshape(MOBA_BLOCK // 8, 8, tq), axis=0)

    def prepare():
        for blk in range(nblk):
            ksum = jnp.zeros((1, heads * HEAD_DIM), F32)
            for hp in range(ppb):
                pg = blk * ppb + hp
                k_rows = k_ref[0, pg].reshape(heads * HEAD_DIM, PAGE_SIZE).T
                ksum = ksum + jnp.sum(k_rows, axis=0, keepdims=True)
                for hh in range(heads):
                    kb_scr[hh, pg * PAGE_SIZE:(pg + 1) * PAGE_SIZE, :] = (
                        k_rows[:, hh * HEAD_DIM:(hh + 1) * HEAD_DIM].astype(BF16))
            kmean_scr[blk:blk + 1, :] = ksum * (1.0 / MOBA_BLOCK)
            for hh in range(heads):
                vt_scr[hh, blk] = jnp.concatenate(
                    [v_ref[0, blk * ppb + hp, hh] for hp in range(ppb)], axis=1).astype(BF16)

    def stage1():
        pl.when(i == 0)(prepare)
        kmean = kmean_scr[...]
        blk_iota = lax.broadcasted_iota(jnp.int32, (nblk, tq), 0)
        past = blk_iota < i
        causal = (lax.broadcasted_iota(jnp.int32, (MOBA_BLOCK, tq), 0)
                  <= lax.broadcasted_iota(jnp.int32, (MOBA_BLOCK, tq), 1))
        qbs = []
        for hh in range(heads):
            q_h = q_ref[0, :, hh * HEAD_DIM:(hh + 1) * HEAD_DIM]
            s_blk = _dot_nt_split(kmean[:, hh * HEAD_DIM:(hh + 1) * HEAD_DIM], q_h)
            s_blk = jnp.where(past, s_blk, -jnp.inf)
            rank = jnp.zeros((nblk, tq), jnp.int32)
            for mm in range(nblk):
                sm = s_blk[mm:mm + 1, :]
                rank = rank + jnp.where(sm > s_blk, 1, jnp.where((sm == s_blk) & (blk_iota > mm), 1, 0))
            bias_scr[hh] = jnp.where(past & (rank < MOBA_TOPK), 0.0, NEG)
            qbs.append((q_h * SCORE_SCALE).astype(BF16))

        def scores(hh, j):
            kj = kb_scr[hh, pl.ds(pl.multiple_of(j * MOBA_BLOCK, MOBA_BLOCK), MOBA_BLOCK), :]
            return _dot_nt(kj, qbs[hh])

        mx_own = []
        for hh in range(heads):
            s_t = jnp.where(causal, scores(hh, i), NEG)
            s_scr[hh, i] = s_t
            mx_own.append(fold(s_t, jnp.max))

        def score_blocks(jj, mx):
            out = list(mx)
            for j in (2 * jj, jnp.minimum(2 * jj + 1, i - 1)):
                for hh in range(heads):
                    s_t = scores(hh, j) + bias_scr[hh, pl.ds(j, 1), :]
                    s_scr[hh, j] = s_t
                    out[hh] = jnp.maximum(out[hh], fold(s_t, jnp.max))
            return tuple(out)

        mx = lax.fori_loop(0, (i + 1) // 2, score_blocks, tuple(mx_own))
        for hh in range(heads):
            pmx_scr[hh] = mx[hh]

    def stage2():
        m_fin = [jnp.max(pmx_scr[hh], axis=0, keepdims=True) for hh in range(heads)]
        for hh in range(heads):
            acc_scr[hh] = jnp.zeros((HEAD_DIM, tq), F32)

        def apply_blocks(jj, ls):
            out = list(ls)
            for j, live in ((2 * jj, True), (jnp.minimum(2 * jj + 1, i), 2 * jj + 1 <= i)):
                for hh in range(heads):
                    shift = m_fin[hh] if live is True else jnp.where(live, m_fin[hh], jnp.inf)
                    p = jnp.exp2(s_scr[hh, j] - shift)
                    out[hh] = out[hh] + fold(p, jnp.sum)
                    acc_scr[hh] = acc_scr[hh] + _dot(vt_scr[hh, j], p.astype(BF16))
            return tuple(out)

        ls = lax.fori_loop(0, i // 2 + 1, apply_blocks, tuple(jnp.zeros((8, tq), F32) for _ in range(heads)))
        o_ref[0] = jnp.concatenate(
            [acc_scr[hh] / jnp.sum(ls[hh], axis=0, keepdims=True) for hh in range(heads)], axis=0).T

    return stage1, stage2


def _prompt_scratch(s, tq, heads):
    nblk = s // MOBA_BLOCK
    return [pltpu.VMEM((heads, s, HEAD_DIM), BF16),
            pltpu.VMEM((heads, nblk, HEAD_DIM, MOBA_BLOCK), BF16),
            pltpu.VMEM((nblk, heads * HEAD_DIM), F32),
            pltpu.VMEM((heads, nblk, tq), F32),
            pltpu.VMEM((heads, nblk, MOBA_BLOCK, tq), F32),
            pltpu.VMEM((heads, HEAD_DIM, tq), F32),
            pltpu.VMEM((heads, 8, tq), F32)]


def _sample_stream(slot, pt_ref, q_ref, kn_ref, vn_ref, ck_hbm, cv_hbm, o_ref,
                   kbuf, vbuf, ksem, vsem, qbd_scr, s_scr, mx_scr, sb_scr, p_scr, pown_scr, l_scr, acc_scr,
                   *, n_seq, n_chunks, pages_per_step):
    ds = q_ref.shape[1]
    rows = N_HEADS * ds
    ppb = MOBA_BLOCK // PAGE_SIZE
    n_pages = n_chunks * pages_per_step
    n_blocks = n_pages // ppb
    n_global = (n_seq + 1) * n_chunks
    lookahead = SAMPLE_RING_SLOTS - 1
    lane = lax.broadcasted_iota(jnp.int32, (rows, 128), 1)
    row = lax.broadcasted_iota(jnp.int32, (rows, 128), 0)

    def page_copies(g, start):
        g_slot = g // n_chunks
        g_c = g % n_chunks
        ring = g % SAMPLE_RING_SLOTS

        def each(cache, buf, sem, seq):
            for p in range(pages_per_step):
                if start:
                    page = pt_ref[seq * n_pages + g_c * pages_per_step + p]
                    pltpu.make_async_copy(cache.at[page], buf.at[ring, p], sem.at[ring]).start()
                else:
                    pltpu.make_async_copy(cache.at[0], buf.at[ring, p], sem.at[ring]).wait()

        @pl.when(g_slot < n_seq)
        def _():
            each(ck_hbm, kbuf, ksem, g_slot)

        @pl.when(g_slot > 0)
        def _():
            each(cv_hbm, vbuf, vsem, g_slot - 1)

    def prime():
        for g in range(lookahead):
            page_copies(jnp.int32(g), start=True)

    def scaled_q():
        return (qbd_scr[...] * SCORE_SCALE).astype(BF16)

    def finalize():
        sc = jnp.where(lane < n_blocks, sb_scr[...], -jnp.inf)
        lane_f = lane.astype(F32)
        sel = jnp.zeros((rows, 128), F32)
        for _ in range(min(MOBA_TOPK, n_blocks)):
            top = jnp.max(sc, axis=1, keepdims=True)
            first = jnp.min(jnp.where(sc == top, lane_f, 1e9), axis=1, keepdims=True)
            pick = lane_f == first
            sel = jnp.where(pick, 1.0, sel)
            sc = jnp.where(pick, -jnp.inf, sc)
        kn = jnp.concatenate([kn_ref[0], jnp.zeros((128 - ds, ATTN_WIDTH), F32)], axis=0)
        s_own = jnp.where(lane <= row % ds, _dot_nt(scaled_q(), kn.astype(BF16)), NEG)
        m_sel = jnp.max(jnp.where(sel > 0, mx_scr[...], NEG), axis=1, keepdims=True)
        m_fin = jnp.maximum(m_sel, jnp.max(s_own, axis=1, keepdims=True))
        p_own = jnp.exp2(s_own - m_fin)
        pown_scr[...] = p_own
        p_sum = p_own
        for blk in range(n_blocks):
            shift = jnp.where(sel[:, blk:blk + 1] > 0, m_fin, jnp.inf)
            p = jnp.exp2(s_scr[blk] - shift)
            p_scr[blk] = p.astype(BF16)
            for hp in range(ppb):
                p_sum = p_sum + p[:, hp * PAGE_SIZE:(hp + 1) * PAGE_SIZE]
        l_scr[...] = jnp.broadcast_to(jnp.sum(p_sum, axis=1, keepdims=True), l_scr.shape)
        acc_scr[...] = jnp.zeros(acc_scr.shape, F32)

    def load_queries():
        q = q_ref[0]
        r_i = lax.broadcasted_iota(jnp.int32, (rows, ATTN_WIDTH), 0)
        l_i = lax.broadcasted_iota(jnp.int32, (rows, ATTN_WIDTH), 1)
        qbd_scr[...] = jnp.where(r_i // ds == l_i // HEAD_DIM, jnp.concatenate([q] * N_HEADS, axis=0), 0.0)
        mx_scr[...] = jnp.full(mx_scr.shape, NEG, F32)
        sb_scr[...] = jnp.zeros(sb_scr.shape, F32)

    def block_operand(buf, ring, bb):
        return jnp.concatenate(
            [buf[ring, bb * ppb + hp].reshape(ATTN_WIDTH, PAGE_SIZE).astype(BF16) for hp in range(ppb)], axis=1)

    def score_k_pages(c, ring):
        qb = scaled_q()
        mx = mx_scr[...]
        sb = sb_scr[...]
        for bb in range(pages_per_step // ppb):
            blk = c * (pages_per_step // ppb) + bb
            sc = _dot(qb, block_operand(kbuf, ring, bb))
            s_scr[blk] = sc
            mx = jnp.where(lane == blk, jnp.max(sc, axis=1, keepdims=True), mx)
            sb = jnp.where(lane == blk, jnp.sum(sc, axis=1, keepdims=True), sb)
        mx_scr[...] = mx
        sb_scr[...] = sb

    def apply_v_pages(c, ring):
        acc = acc_scr[...]
        for bb in range(pages_per_step // ppb):
            p = p_scr[c * (pages_per_step // ppb) + bb]
            acc = acc + _dot_nt(p, block_operand(vbuf, ring, bb))
        acc_scr[...] = acc

    def chunk(c, carry):
        g = slot * n_chunks + c
        ring = g % SAMPLE_RING_SLOTS

        @pl.when(g + lookahead < n_global)
        def _():
            page_copies(g + lookahead, start=True)

        page_copies(g, start=False)

        @pl.when((slot > 0) & (slot < n_seq))
        def _():
            score_k_pages(c, ring)
            apply_v_pages(c, ring)

        @pl.when(slot == 0)
        def _():
            score_k_pages(c, ring)

        @pl.when(slot == n_seq)
        def _():
            apply_v_pages(c, ring)

        return carry

    def write_output():
        vn = jnp.concatenate([vn_ref[0], jnp.zeros((128 - ds, ATTN_WIDTH), F32)], axis=0)
        full = acc_scr[...] + _dot(pown_scr[...].astype(BF16), vn.astype(BF16))
        full = full / l_scr[:, 0:1]
        l_i = lax.broadcasted_iota(jnp.int32, (ds, ATTN_WIDTH), 1)
        out = jnp.zeros((ds, ATTN_WIDTH), F32)
        for hh in range(N_HEADS):
            out = out + jnp.where(l_i // HEAD_DIM == hh, full[hh * ds:(hh + 1) * ds, :], 0.0)
        o_ref[0] = out

    def begin():
        pl.when(slot == 0)(prime)
        pl.when(slot > 0)(finalize)
        pl.when(slot < n_seq)(load_queries)

    def run_chunks(c0, c1):
        lax.fori_loop(c0, c1, chunk, 0)

    def end():
        pl.when(slot > 0)(write_output)

    return begin, run_chunks, end


def _sample_scratch(ds, n_pages, pps):
    rows = N_HEADS * ds
    n_blocks = n_pages * PAGE_SIZE // MOBA_BLOCK
    ring_buf = pltpu.VMEM((SAMPLE_RING_SLOTS, pps, N_HEADS, HEAD_DIM, PAGE_SIZE), F32)
    return [ring_buf, ring_buf,
            pltpu.SemaphoreType.DMA((SAMPLE_RING_SLOTS,)),
            pltpu.SemaphoreType.DMA((SAMPLE_RING_SLOTS,)),
            pltpu.VMEM((rows, ATTN_WIDTH), F32),
            pltpu.VMEM((n_blocks, rows, MOBA_BLOCK), F32),
            pltpu.VMEM((rows, 128), F32),
            pltpu.VMEM((rows, 128), F32),
            pltpu.VMEM((n_blocks, rows, MOBA_BLOCK), BF16),
            pltpu.VMEM((rows, 128), F32),
            pltpu.VMEM((rows, 128), F32),
            pltpu.VMEM((rows, ATTN_WIDTH), F32)]


def _moba_body(pt_ref, qs_ref, kn_ref, vn_ref, ck_hbm, cv_hbm, qp_ref, kp_ref, vp_ref, os_ref, op_ref, *scratch,
               n_seq, n_chunks, pages_per_step, n_prompt_steps, n_qtiles, n_sample_scratch):
    t = pl.program_id(0)
    begin, run_chunks, end = _sample_stream(
        t, pt_ref, qs_ref, kn_ref, vn_ref, ck_hbm, cv_hbm, os_ref, *scratch[:n_sample_scratch],
        n_seq=n_seq, n_chunks=n_chunks, pages_per_step=pages_per_step)
    stage1, stage2 = _prompt_attention_stages(t % n_qtiles, qp_ref, kp_ref, vp_ref, op_ref,
                                              *scratch[n_sample_scratch:])
    n_steps = max(n_seq + 1, n_prompt_steps)
    sample_part = (lambda f: f()) if n_steps == n_seq + 1 else (lambda f: pl.when(t <= n_seq)(f))
    prompt_part = (lambda f: f()) if n_steps == n_prompt_steps else (lambda f: pl.when(t < n_prompt_steps)(f))
    c1, c2 = n_chunks // 4, (5 * n_chunks) // 8

    def sample_head():
        begin()
        run_chunks(0, c1)

    def sample_middle():
        run_chunks(c1, c2)

    def sample_tail():
        run_chunks(c2, n_chunks)
        end()

    sample_part(sample_head)
    prompt_part(stage1)
    sample_part(sample_middle)
    prompt_part(stage2)
    sample_part(sample_tail)


def _moba(q_p, k_paged, v_paged, q_s, k_new, v_new, cache_k, cache_v, page_table):
    b, s, w = q_p.shape
    db, ds, _ = q_s.shape
    n_pages = page_table.shape[1]
    assert (n_pages * PAGE_SIZE) % MOBA_BLOCK == 0, "past length must be whole MoBA blocks"
    pps = SAMPLE_PAGES_PER_STEP
    n_chunks = n_pages // pps
    assert n_chunks * pps == n_pages and pps % (MOBA_BLOCK // PAGE_SIZE) == 0
    ck = jnp.transpose(cache_k, (0, 2, 3, 1))
    cv = jnp.transpose(cache_v, (0, 2, 3, 1))
    pt = page_table.reshape(-1)

    tq = MOBA_BLOCK
    heads = PROMPT_HEADS_PER_STEP
    lanes = heads * HEAD_DIM
    n_qt, n_hg = s // tq, w // lanes
    n_prompt = b * n_hg * n_qt
    n_steps = max(db + 1, n_prompt)

    def prompt_index(t):
        tp = jnp.minimum(t, n_prompt - 1)
        return tp // (n_hg * n_qt), (tp // n_qt) % n_hg, tp % n_qt

    def q_map(t, pt_ref):
        bi, hg, i = prompt_index(t)
        return (bi, i, hg)

    def kv_map(t, pt_ref):
        bi, hg, _ = prompt_index(t)
        return (bi, 0, hg, 0, 0)

    cur = pl.BlockSpec((1, ds, w), lambda t, pt_ref: (jnp.minimum(t, db - 1), 0, 0))
    prev = pl.BlockSpec((1, ds, w), lambda t, pt_ref: (jnp.clip(t - 1, 0, db - 1), 0, 0))
    hbm = pl.BlockSpec(memory_space=pl.ANY)
    q_spec = pl.BlockSpec((1, tq, lanes), q_map)
    kv_spec = pl.BlockSpec((1, s // PAGE_SIZE, heads, HEAD_DIM, PAGE_SIZE), kv_map)
    sample_scratch = _sample_scratch(ds, n_pages, pps)
    grid_spec = pltpu.PrefetchScalarGridSpec(
        num_scalar_prefetch=1,
        grid=(n_steps,),
        in_specs=[cur, prev, prev, hbm, hbm, q_spec, kv_spec, kv_spec],
        out_specs=[prev, q_spec],
        scratch_shapes=sample_scratch + _prompt_scratch(s, tq, heads))
    body = functools.partial(_moba_body, n_seq=db, n_chunks=n_chunks, pages_per_step=pps,
                             n_prompt_steps=n_prompt, n_qtiles=n_qt, n_sample_scratch=len(sample_scratch))
    y_s, y_p = pl.pallas_call(
        body,
        grid_spec=grid_spec,
        out_shape=[jax.ShapeDtypeStruct((db, ds, w), F32), jax.ShapeDtypeStruct((b, s, w), F32)],
        compiler_params=pltpu.CompilerParams(dimension_semantics=("arbitrary",),
                                             vmem_limit_bytes=VMEM_LIMIT),
        name="moba",
    )(pt, q_s, k_new, v_new, ck, cv, q_p, k_paged, v_paged)
    return y_p, y_s


def _mixer_out_body(x_ref, yc_ref, ya_ref, ag_ref, gmc_ref, gma_ref, gate_ref, wb_ref, wo_ref, fg_ref, o_ref):
    gb, tb, d = x_ref.shape
    m = gb * tb
    yc = yc_ref[...].reshape(m, CONV_CH).astype(BF16)
    ya = (ya_ref[...] * _silu(ag_ref[...])).reshape(m, ATTN_WIDTH).astype(BF16)
    merged = (jax.nn.sigmoid(gmc_ref[...].reshape(m, d)) * _dot(yc, wb_ref[0])
              + jax.nn.sigmoid(gma_ref[...].reshape(m, d)) * _dot(ya, wb_ref[1]))
    o = _dot(merged.astype(BF16), wo_ref[...]).reshape(gb, tb, d)
    xo = x_ref[...] + gate_ref[...] * o
    r = xo * lax.rsqrt(jnp.mean(xo * xo, axis=-1, keepdims=True) + EPS)
    o_ref[...] = r * fg_ref[...]


def _mixer_out(x3, y_conv, y_attn, ag, gmc, gma, gate, wb_bf, wo_bf, final_g, gb, tb):
    g, t, d = x3.shape
    xmap = lambda i, j: (i, j, 0)
    bmap = lambda i, j: (i, 0, 0)
    spec = lambda w: pl.BlockSpec((gb, tb, w), xmap)
    return pl.pallas_call(
        _mixer_out_body,
        grid=(g // gb, t // tb),
        in_specs=[spec(d), spec(CONV_CH), spec(ATTN_WIDTH), spec(ATTN_WIDTH), spec(d), spec(d),
                  pl.BlockSpec((gb, 1, d), bmap),
                  pl.BlockSpec((2, CONV_CH, d), lambda i, j: (0, 0, 0)),
                  pl.BlockSpec((d, d), lambda i, j: (0, 0)),
                  pl.BlockSpec((1, d), lambda i, j: (0, 0))],
        out_specs=spec(d),
        out_shape=jax.ShapeDtypeStruct((g, t, d), F32),
        compiler_params=pltpu.CompilerParams(dimension_semantics=("parallel", "parallel"),
                                             vmem_limit_bytes=VMEM_LIMIT),
        name="mixer_out",
    )(x3, y_conv, y_attn, ag, gmc, gma, gate, wb_bf, wo_bf, final_g)


def kernel(x_prompt, x_sample, cache_k, cache_v, state_conv, page_table, c_prompt, c_sample, norm_g, w_ada,
           b_ada, w_in, conv_w, conv_b, conv_ln_g, conv_ln_b, w_branch, w_out, final_g):
    depth = norm_g.shape[0]
    assert depth == 1, "single-layer trunk"
    b, s, d = x_prompt.shape
    db, ds, _ = x_sample.shape
    assert CONV_CH == ATTN_WIDTH and s % MOBA_BLOCK == 0 and s >= CONV_WIDTH - 1
    past_len = page_table.shape[1] * PAGE_SIZE
    lyr = 0

    ada = _ada(jnp.concatenate([c_prompt, c_sample], axis=0), w_ada[lyr], b_ada[lyr])
    shift, scale, gate = (ada[:, j * d:(j + 1) * d][:, None, :] for j in range(3))
    w_in_bf = w_in[lyr].astype(BF16)
    wb_bf = w_branch[lyr].astype(BF16)
    wo_bf = w_out[lyr].astype(BF16)
    g_in = norm_g[lyr].reshape(1, d)
    g_fin = final_g.reshape(1, d)
    cw, cb = conv_w[lyr], conv_b[lyr].reshape(1, CONV_CH)
    lg, lb = conv_ln_g[lyr].reshape(1, CONV_CH), conv_ln_b[lyr].reshape(1, CONV_CH)

    tab_p = _rope_tables(jnp.arange(s))
    yc_p, glu_tail, q_p, k_p, v_p, ag_p, gmc_p, gma_p = _mixer_in(
        x_prompt, scale[:b], shift[:b], g_in, w_in_bf, tab_p, 1, ROW_TILE, conv_params=(cw, cb, lg, lb))
    gbs = ROW_TILE // ds
    tab_s = tuple(jnp.tile(a, (gbs, 1)) for a in _rope_tables(past_len + jnp.arange(ds)))
    glu_s, cg_s, q_s, k_s, v_s, ag_s, gmc_s, gma_s = _mixer_in(
        x_sample, scale[b:], shift[b:], g_in, w_in_bf, tab_s, gbs, ds)
    yc_s, state_next = _conv_sample(jnp.transpose(state_conv[lyr], (1, 0, 2)), glu_s, cg_s, cw, cb, lg, lb)

    ya_p, ya_s = _moba(q_p, k_p, v_p, q_s, k_s, v_s, cache_k[lyr], cache_v[lyr], page_table)

    y_prompt = _mixer_out(x_prompt, yc_p, ya_p, ag_p, gmc_p, gma_p, gate[:b], wb_bf, wo_bf, g_fin, 1, OUT_ROW_TILE)
    y_sample = _mixer_out(x_sample, yc_s, ya_s, ag_s, gmc_s, gma_s, gate[b:], wb_bf, wo_bf, g_fin,
                          OUT_ROW_TILE // ds, ds)
    k_prompt = jnp.transpose(k_p, (0, 1, 4, 2, 3))[None]
    v_prompt = jnp.transpose(v_p, (0, 1, 4, 2, 3))[None]
    conv_prompt = glu_tail[None, :, HIST_PAD - (CONV_WIDTH - 1):, :]
    k_sample = k_s.reshape(1, db, ds, N_HEADS, HEAD_DIM)
    v_sample = v_s.reshape(1, db, ds, N_HEADS, HEAD_DIM)
    conv_sample = jnp.transpose(state_next, (1, 0, 2))[None]

    return (y_prompt, y_sample, k_prompt, v_prompt, conv_prompt, k_sample, v_sample, conv_sample)
```
